```python
import jax, jax.numpy as jnp
from jax import lax
import numpy as np

D_MODEL = 1024
BATCH = 8
SEQ = 4096
DEPTH = 4
DEC_BATCH = 32
DEC_SEQ = 64
PAST_LEN = 2048

CHUNK = 64
HEAD_DIM = 64
SWA_HEADS = 8
SWA_KV_HEADS = 2
SWA_GROUP = SWA_HEADS // SWA_KV_HEADS
WINDOW = 128
WINDOW_CHUNKS = WINDOW // CHUNK
SWA_Q = SWA_HEADS * HEAD_DIM
SWA_KV = SWA_KV_HEADS * HEAD_DIM
GMLP_GROUPS = 4
GMLP_BLOCK = 128
GMLP_WIDTH = GMLP_GROUPS * HEAD_DIM
RET_HEADS = 4
RET_WIDTH = RET_HEADS * HEAD_DIM
ROPE_BASE = 10000.0
IN_SPLITS = (SWA_Q, SWA_KV, SWA_KV, GMLP_WIDTH, GMLP_WIDTH, RET_WIDTH, RET_WIDTH, RET_WIDTH, RET_WIDTH)
IN_WIDTH = sum(IN_SPLITS)
MIX_WIDTH = SWA_Q + GMLP_WIDTH + RET_WIDTH
D_FF = ((8 * D_MODEL + 3 * 256 - 1) // (3 * 256)) * 256
EPS = 1e-6
NEG = -1e30

kernel_name = 'hybrid_swa_gmlp_retention_stream_step'


def rms_norm(x, g):
    xf = x.astype(jnp.float32)
    y = xf * lax.rsqrt(jnp.mean(xf * xf, axis=-1, keepdims=True) + EPS)
    return (y * g.astype(jnp.float32)).astype(x.dtype)


def layer_norm(x, g, b):
    xf = x.astype(jnp.float32)
    mu = jnp.mean(xf, axis=-1, keepdims=True)
    var = jnp.mean(jnp.square(xf - mu), axis=-1, keepdims=True)
    y = (xf - mu) * lax.rsqrt(var + EPS)
    return (y * g.astype(jnp.float32) + b.astype(jnp.float32)).astype(x.dtype)


def project(h, w_in):
    y = h @ w_in
    cuts = [int(c) for c in np.cumsum(IN_SPLITS)[:-1]]
    return jnp.split(y, cuts, axis=-1)


def sink_attention(q, k, v, sinks, mask):
    s = jnp.einsum('...lhgd,...mhd->...hglm', q, k).astype(jnp.float32) * (HEAD_DIM ** -0.5)
    if mask is not None:
        s = jnp.where(mask, s, NEG)
    sink = jnp.broadcast_to(sinks.astype(jnp.float32)[..., None, None], s.shape[:-1] + (1,))
    p = jax.nn.softmax(jnp.concatenate([s, sink], axis=-1), axis=-1)[..., :-1]
    return jnp.einsum('...hglm,...mhd->...lhgd', p.astype(v.dtype), v)


def swa_prompt(q, k, v, sinks):
    B, S = q.shape[:2]
    n = S // CHUNK
    qc = q.reshape(B, n, CHUNK, SWA_KV_HEADS, SWA_GROUP, HEAD_DIM)
    pad = ((0, 0), (WINDOW_CHUNKS * CHUNK, 0), (0, 0), (0, 0))
    kp = jnp.pad(k, pad).reshape(B, n + WINDOW_CHUNKS, CHUNK, SWA_KV_HEADS, HEAD_DIM)
    vp = jnp.pad(v, pad).reshape(B, n + WINDOW_CHUNKS, CHUNK, SWA_KV_HEADS, HEAD_DIM)
    kw = jnp.concatenate([kp[:, j:j + n] for j in range(WINDOW_CHUNKS + 1)], axis=2)
    vw = jnp.concatenate([vp[:, j:j + n] for j in range(WINDOW_CHUNKS + 1)], axis=2)
    key_pos = (jnp.arange(n)[:, None] * CHUNK + jnp.arange((WINDOW_CHUNKS + 1) * CHUNK)[None, :]
               - WINDOW_CHUNKS * CHUNK)
    mask = (key_pos >= 0)[:, None, None, None, :]
    o = sink_attention(qc, kw, vw, sinks, mask)
    return o.reshape(B, S, SWA_Q)


def swa_sample(q, k, v, cache_k, cache_v, sinks):
    B, T = q.shape[:2]
    qh = q.reshape(B, T, SWA_KV_HEADS, SWA_GROUP, HEAD_DIM)
    kk = jnp.concatenate([cache_k.astype(k.dtype), k], axis=1)
    vv = jnp.concatenate([cache_v.astype(v.dtype), v], axis=1)
    o = sink_attention(qh, kk, vv, sinks, None)
    return o.reshape(B, T, SWA_Q)


def gmlp_mix(u, v, w_s, b_s, ln_g, ln_b):
    B, S, _ = u.shape
    L = min(S, GMLP_BLOCK)
    n = S // L
    u = jax.nn.gelu(u)
    v = layer_norm(jax.nn.gelu(v), ln_g, ln_b)
    i = jnp.arange(L)
    mask = (i[:, None] // CHUNK) >= (i[None, :] // CHUNK)
    w = jnp.where(mask, w_s[:, :L, :L], 0.0)
    vb = v.reshape(B, n, L, GMLP_GROUPS, HEAD_DIM)
    mixed = jnp.einsum('gij,bnjgc->bnigc', w, vb) + b_s[:, :L].T[:, :, None]
    return u * mixed.reshape(B, S, GMLP_WIDTH).astype(u.dtype), v


def rotary(x, pos):
    half = HEAD_DIM // 2
    inv = ROPE_BASE ** (-jnp.arange(half, dtype=jnp.float32) / half)
    ang = pos.astype(jnp.float32)[:, None] * inv[None, :]
    cos = jnp.cos(ang)[None, :, None, :]
    sin = jnp.sin(ang)[None, :, None, :]
    x1 = x[..., :half].astype(jnp.float32)
    x2 = x[..., half:].astype(jnp.float32)
    return jnp.concatenate([x1 * cos - x2 * sin, x2 * cos + x1 * sin], axis=-1).astype(x.dtype)


def retention(q, k, v, state0, block):
    B, S, H, d = q.shape
    n = S // block
    logg = jnp.log(1.0 - 2.0 ** (-5.0 - jnp.arange(H, dtype=jnp.float32)))
    idx = jnp.arange(block, dtype=jnp.float32)
    rel = idx[:, None] - idx[None, :]
    decay_mat = jnp.where(rel >= 0, jnp.exp(logg[:, None, None] * jnp.maximum(rel, 0.0)), 0.0)
    xi = jnp.exp(logg[:, None] * (idx + 1.0))
    zeta = jnp.exp(logg[:, None] * (block - 1.0 - idx))
    chunk_decay = jnp.exp(logg * block)[None, :, None, None]
    qc = q.reshape(B, n, block, H, d).astype(jnp.float32)
    kc = k.reshape(B, n, block, H, d).astype(jnp.float32)
    vc = v.reshape(B, n, block, H, d).astype(jnp.float32)
    inner = jnp.einsum('bclhd,bcmhd->bchlm', qc, kc) * decay_mat
    inner = jnp.einsum('bchlm,bcmhe->bclhe', inner, vc)
    kv = jnp.einsum('bcmhd,bcmhe,hm->bchde', kc, vc, zeta)

    def step(state, kv_c):
        return chunk_decay * state + kv_c, state

    final, prev = lax.scan(step, state0.astype(jnp.float32), jnp.moveaxis(kv, 1, 0))
    prev = jnp.moveaxis(prev, 0, 1)
    cross = jnp.einsum('bclhd,bchde,hl->bclhe', qc, prev, xi)
    return (inner + cross).reshape(B, S, H, d), final.astype(state0.dtype)


def head_group_norm(o, g):
    B, S, H, d = o.shape
    mu = jnp.mean(o, axis=-1, keepdims=True)
    var = jnp.mean(jnp.square(o - mu), axis=-1, keepdims=True)
    y = ((o - mu) * lax.rsqrt(var + EPS)).reshape(B, S, H * d)
    return y * g.astype(jnp.float32)


def trunk_layer(x, pos, p, cache_k, cache_v, ret_state):
    B, S, _ = x.shape
    h = rms_norm(x, p['pre_mix'])
    q_a, k_a, v_a, u_b, v_b, q_c, k_c, v_c, g_c = project(h, p['w_in'])
    k_a = k_a.reshape(B, S, SWA_KV_HEADS, HEAD_DIM)
    v_a = v_a.reshape(B, S, SWA_KV_HEADS, HEAD_DIM)
    sinks = p['sinks'].reshape(SWA_KV_HEADS, SWA_GROUP)
    if cache_k is None:
        a = swa_prompt(q_a, k_a, v_a, sinks)
        swa_new = (k_a[:, S - WINDOW:], v_a[:, S - WINDOW:])
        state0 = jnp.zeros((B, RET_HEADS, HEAD_DIM, HEAD_DIM), x.dtype)
        block = CHUNK
    else:
        a = swa_sample(q_a, k_a, v_a, cache_k, cache_v, sinks)
        swa_new = (k_a, v_a)
        state0 = ret_state
        block = S
    b, v_rows = gmlp_mix(u_b, v_b, p['gmlp_w'], p['gmlp_b'], p['gmlp_ln_g'], p['gmlp_ln_b'])
    qr = rotary(q_c.reshape(B, S, RET_HEADS, HEAD_DIM), pos)
    kr = rotary(k_c.reshape(B, S, RET_HEADS, HEAD_DIM), pos) * (HEAD_DIM ** -0.5)
    r, ret_final = retention(qr, kr, v_c.reshape(B, S, RET_HEADS, HEAD_DIM), state0, block)
    c = (head_group_norm(r, p['ret_g']) * jax.nn.silu(g_c.astype(jnp.float32))).astype(x.dtype)
    merged = jnp.concatenate([rms_norm(a, p['norm_a']), rms_norm(b, p['norm_b']), c], axis=-1) @ p['w_out']
    x = x + rms_norm(merged, p['post_mix'])
    h = rms_norm(x, p['pre_ffn'])
    f = (jax.nn.silu(h @ p['w_gate']) * (h @ p['w_up'])) @ p['w_down']
    x = x + rms_norm(f, p['post_ffn'])
    return x, swa_new, ret_final, v_rows


def setup_inputs(seed: int = 0) -> dict:
    key = jax.random.key(seed)
    ks = jax.random.split(key, 24)

    def nrm(k, shape, scale):
        return scale * jax.random.normal(k, shape, jnp.float32)

    swa_cache = min(WINDOW, PAST_LEN)
    return {
        'x_prompt': nrm(ks[0], (BATCH, SEQ, D_MODEL), 1.0),
        'x_sample': nrm(ks[1], (DEC_BATCH, DEC_SEQ, D_MODEL), 1.0),
        'cache_swa_k': nrm(ks[2], (DEC_BATCH, DEPTH, swa_cache, SWA_KV_HEADS, HEAD_DIM), 1.0),
        'cache_swa_v': nrm(ks[3], (DEC_BATCH, DEPTH, swa_cache, SWA_KV_HEADS, HEAD_DIM), 1.0),
        'state_ret': nrm(ks[4], (DEC_BATCH, DEPTH, RET_HEADS, HEAD_DIM, HEAD_DIM), 1.0),
        'w_in': nrm(ks[5], (DEPTH, D_MODEL, IN_WIDTH), D_MODEL ** -0.5),
        'w_out': nrm(ks[6], (DEPTH, MIX_WIDTH, D_MODEL), MIX_WIDTH ** -0.5),
        'swa_sinks': nrm(ks[7], (DEPTH, SWA_HEADS), 0.5),
        'gmlp_w': nrm(ks[8], (DEPTH, GMLP_GROUPS, GMLP_BLOCK, GMLP_BLOCK), GMLP_BLOCK ** -0.5),
        'gmlp_b': 1.0 + nrm(ks[9], (DEPTH, GMLP_GROUPS, GMLP_BLOCK), 0.1),
        'gmlp_ln_g': 1.0 + nrm(ks[10], (DEPTH, GMLP_WIDTH), 0.1),
        'gmlp_ln_b': nrm(ks[11], (DEPTH, GMLP_WIDTH), 0.02),
        'norm_a_g': 1.0 + nrm(ks[12], (DEPTH, SWA_Q), 0.1),
        'norm_b_g': 1.0 + nrm(ks[13], (DEPTH, GMLP_WIDTH), 0.1),
        'ret_norm_g': 1.0 + nrm(ks[14], (DEPTH, RET_WIDTH), 0.1),
        'ln_pre_mix': 1.0 + nrm(ks[15], (DEPTH, D_MODEL), 0.1),
        'ln_post_mix': 1.0 + nrm(ks[16], (DEPTH, D_MODEL), 0.1),
        'ln_pre_ffn': 1.0 + nrm(ks[17], (DEPTH, D_MODEL), 0.1),
        'ln_post_ffn': 1.0 + nrm(ks[18], (DEPTH, D_MODEL), 0.1),
        'w_gate': nrm(ks[19], (DEPTH, D_MODEL, D_FF), D_MODEL ** -0.5),
        'w_up': nrm(ks[20], (DEPTH, D_MODEL, D_FF), D_MODEL ** -0.5),
        'w_down': nrm(ks[21], (DEPTH, D_FF, D_MODEL), D_FF ** -0.5),
    }


def reference(x_prompt, x_sample, cache_swa_k, cache_swa_v, state_ret, w_in, w_out, swa_sinks,
              gmlp_w, gmlp_b, gmlp_ln_g, gmlp_ln_b, norm_a_g, norm_b_g, ret_norm_g,
              ln_pre_mix, ln_post_mix, ln_pre_ffn, ln_post_ffn, w_gate, w_up, w_down):
    pos_p = jnp.arange(x_prompt.shape[1], dtype=jnp.int32)
    pos_s = PAST_LEN + jnp.arange(x_sample.shape[1], dtype=jnp.int32)
    xp, xs = x_prompt, x_sample
    kp_l, vp_l, ks_l, vs_l, rp_l, rs_l, gv_l = [], [], [], [], [], [], []
    for l in range(DEPTH):
        p = {
            'w_in': w_in[l], 'w_out': w_out[l], 'sinks': swa_sinks[l],
            'gmlp_w': gmlp_w[l], 'gmlp_b': gmlp_b[l], 'gmlp_ln_g': gmlp_ln_g[l], 'gmlp_ln_b': gmlp_ln_b[l],
            'norm_a': norm_a_g[l], 'norm_b': norm_b_g[l], 'ret_g': ret_norm_g[l],
            'pre_mix': ln_pre_mix[l], 'post_mix': ln_post_mix[l],
            'pre_ffn': ln_pre_ffn[l], 'post_ffn': ln_post_ffn[l],
            'w_gate': w_gate[l], 'w_up': w_up[l], 'w_down': w_down[l],
        }
        xp, (kp, vp), rp, _ = trunk_layer(xp, pos_p, p, None, None, None)
        xs, (ks, vs), rs, gv = trunk_layer(xs, pos_s, p, cache_swa_k[:, l], cache_swa_v[:, l], state_ret[:, l])
        kp_l.append(kp); vp_l.append(vp); ks_l.append(ks); vs_l.append(vs)
        rp_l.append(rp); rs_l.append(rs); gv_l.append(gv)
    return (xp, xs, jnp.stack(kp_l, axis=1), jnp.stack(vp_l, axis=1), jnp.stack(ks_l, axis=1),
            jnp.stack(vs_l, axis=1), jnp.stack(rp_l, axis=1), jnp.stack(rs_l, axis=1), jnp.stack(gv_l, axis=1))
```

```python
import functools
import math

import numpy as np
import jax
import jax.numpy as jnp
from jax import lax
from jax.experimental import pallas as pl
from jax.experimental.pallas import tpu as pltpu

D_MODEL = 1024
DEPTH = 4
PAST_LEN = 2048
CHUNK = 64
HEAD_DIM = 64
SWA_HEADS = 8
SWA_KV_HEADS = 2
WINDOW = 128
SWA_Q = SWA_HEADS * HEAD_DIM
SWA_KV = SWA_KV_HEADS * HEAD_DIM
GMLP_GROUPS = 4
GMLP_BLOCK = 128
GMLP_WIDTH = GMLP_GROUPS * HEAD_DIM
RET_HEADS = 4
RET_WIDTH = RET_HEADS * HEAD_DIM
ROPE_BASE = 10000.0
IN_WIDTH = SWA_Q + 2 * SWA_KV + 2 * GMLP_WIDTH + 4 * RET_WIDTH
D_FF = 2816
EPS = 1e-6
NEG = -1e30

OFF_QA, OFF_KA, OFF_VA, OFF_UB, OFF_VB, OFF_QC, OFF_KC, OFF_VC, OFF_GC = (
    0, 512, 640, 768, 1024, 1280, 1536, 1792, 2048)

LANES = 128
PROMPT_BLOCK = 128
PROMPT_TILE = 256
SAMPLE_SEQS = 4
VMEM_LIMIT_BYTES = 56 * 1024 * 1024

HEADS_PLAIN = (0, 2, 5, 7)
HEADS_SWAPPED = (1, 3, 4, 6)

F32 = jnp.float32
BF16 = jnp.bfloat16


def _dot(a, b):
    return jnp.dot(a, b, preferred_element_type=F32)


def _dot_nt(a, b):
    return lax.dot_general(a, b, (((1,), (1,)), ((), ())), preferred_element_type=F32)


def _rms(x, g):
    return x * lax.rsqrt(jnp.mean(x * x, axis=-1, keepdims=True) + EPS) * g


def _gelu(x):
    c = math.sqrt(2.0 / math.pi)
    return x * (0.5 * (1.0 + jnp.tanh(c * (x + 0.044715 * (x * x * x)))))


def _silu(x):
    return x / (1.0 + jnp.exp(-x))


def _swa_block(q, kw, kw_sw, vw, vw_sw, sink_ref, bias):
    rows = q.shape[0]
    lo = lax.broadcasted_iota(jnp.int32, (rows, LANES), 1) < HEAD_DIM
    scale = HEAD_DIM ** -0.5

    def half(pair, keep_lo):
        qp = q[:, pair * LANES:(pair + 1) * LANES] * scale
        return jnp.where(lo if keep_lo else jnp.logical_not(lo), qp, 0.0).astype(BF16)

    lhs_plain = jnp.concatenate([half(0, True), half(1, True), half(2, False), half(3, False)], axis=0)
    lhs_swap = jnp.concatenate([half(0, False), half(1, False), half(2, True), half(3, True)], axis=0)

    def attend(lhs, k, v, heads):
        s = _dot_nt(lhs, k)
        if bias is not None:
            s = s + bias
        sink = jnp.concatenate([jnp.full((rows, 1), sink_ref[h], F32) for h in heads], axis=0)
        m = jnp.maximum(jnp.max(s, axis=-1, keepdims=True), sink)
        p = jnp.exp(s - m)
        den = jnp.sum(p, axis=-1, keepdims=True) + jnp.exp(sink - m)
        return _dot(p.astype(BF16), v) / den

    o_p = attend(lhs_plain, kw, vw, HEADS_PLAIN)
    o_s = attend(lhs_swap, kw_sw, vw_sw, HEADS_SWAPPED)
    r = rows
    pairs = [
        jnp.where(lo, o_p[0:r], o_s[0:r]),
        jnp.where(lo, o_p[r:2 * r], o_s[r:2 * r]),
        jnp.where(lo, o_s[2 * r:3 * r], o_p[2 * r:3 * r]),
        jnp.where(lo, o_s[3 * r:4 * r], o_p[3 * r:4 * r]),
    ]
    return jnp.concatenate(pairs, axis=1)


def _gmlp_block(u_raw, v_raw, w_stack, bias_tab, ln_g, ln_b):
    rows = u_raw.shape[0]
    u = _gelu(u_raw)
    v = _gelu(v_raw)
    mu = jnp.mean(v, axis=-1, keepdims=True)
    d = v - mu
    var = jnp.mean(d * d, axis=-1, keepdims=True)
    vn = d * lax.rsqrt(var + EPS) * ln_g + ln_b
    lo = lax.broadcasted_iota(jnp.int32, (rows, LANES), 1) < HEAD_DIM
    vb = vn.astype(BF16)
    mixed = []
    for j in range(GMLP_GROUPS // 2):
        res = _dot(w_stack[j], vb[:, j * LANES:(j + 1) * LANES])
        mixed.append(jnp.where(lo, res[0:rows], res[rows:2 * rows]))
    mixed = jnp.concatenate(mixed, axis=1) + bias_tab
    return u * mixed, vn


def _rotary(x, cos_t, sin_t):
    rows = x.shape[0]
    first = (lax.broadcasted_iota(jnp.int32, (rows, LANES), 1) % HEAD_DIM) < (HEAD_DIM // 2)
    out = []
    for j in range(RET_WIDTH // LANES):
        sl = slice(j * LANES, (j + 1) * LANES)
        xh = x[:, sl]
        partner = jnp.where(first, pltpu.roll(xh, LANES - HEAD_DIM // 2, 1), pltpu.roll(xh, HEAD_DIM // 2, 1))
        out.append(xh * cos_t[:, sl] + partner * sin_t[:, sl])
    return jnp.concatenate(out, axis=1)


def _retention_block(qc, kc, vc, gc, cos_t, sin_t, state_bf, d4, xi_tab, zeta_tab, bd_mask, ret_g):
    rows = qc.shape[0]
    qr = _rotary(qc, cos_t, sin_t)
    kr = _rotary(kc, cos_t, sin_t) * (HEAD_DIM ** -0.5)
    head = lax.broadcasted_iota(jnp.int32, (rows, RET_WIDTH), 1) // HEAD_DIM
    q4 = jnp.concatenate([jnp.where(head == h, qr, 0.0) for h in range(RET_HEADS)], axis=0).astype(BF16)
    vb = vc.astype(BF16)
    s4 = _dot_nt(q4, kr.astype(BF16)) * d4
    o4 = _dot(s4.astype(BF16), vb)
    inner = o4[0:rows]
    for h in range(1, RET_HEADS):
        inner = jnp.where(head == h, o4[h * rows:(h + 1) * rows], inner)
    cross = _dot(qr.astype(BF16), state_bf) * xi_tab
    r = inner + cross
    kz_t = jnp.transpose(kr * zeta_tab).astype(BF16)
    kv = _dot(kz_t, vb)
    avg = (bd_mask * (1.0 / HEAD_DIM)).astype(BF16)
    mu = _dot(r.astype(BF16), avg)
    d = r - mu
    var = _dot((d * d).astype(BF16), avg)
    yn = d * lax.rsqrt(var + EPS) * ret_g
    return yn * _silu(gc), kv


def _mix_rows(y_ref, mix_ref, r0, rows, attn_fn, gmlp_fn, ret_fn, g_na, g_nb):
    sl = pl.ds(r0, rows)
    a = attn_fn(y_ref[sl, OFF_QA:OFF_QA + SWA_Q])
    mix_ref[sl, 0:SWA_Q] = _rms(a, g_na).astype(BF16)
    b, vn = gmlp_fn(y_ref[sl, OFF_UB:OFF_UB + GMLP_WIDTH], y_ref[sl, OFF_VB:OFF_VB + GMLP_WIDTH])
    mix_ref[sl, SWA_Q:SWA_Q + GMLP_WIDTH] = _rms(b, g_nb).astype(BF16)
    c, kv = ret_fn(y_ref[sl, OFF_QC:OFF_QC + RET_WIDTH], y_ref[sl, OFF_KC:OFF_KC + RET_WIDTH],
                   y_ref[sl, OFF_VC:OFF_VC + RET_WIDTH], y_ref[sl, OFF_GC:OFF_GC + RET_WIDTH])
    mix_ref[sl, SWA_Q + GMLP_WIDTH:D_MODEL] = c.astype(BF16)
    return vn, kv


def _ffn_tail(x, mix_ref, w_out_ref, wg_ref, wu_ref, wd_ref, g_post_mix, g_pre_ffn, g_post_ffn):
    merged = _dot(mix_ref[...], w_out_ref[...])
    x1 = x + _rms(merged, g_post_mix)
    h2 = _rms(x1, g_pre_ffn).astype(BF16)
    gate = _dot(h2, wg_ref[...])
    up = _dot(h2, wu_ref[...])
    act = (_silu(gate) * up).astype(BF16)
    f = _dot(act, wd_ref[...])
    return x1 + _rms(f, g_post_ffn)


def _store_diag_blocks(out_ref, m):
    for h in range(RET_HEADS):
        out_ref[h] = m[h * HEAD_DIM:(h + 1) * HEAD_DIM, h * HEAD_DIM:(h + 1) * HEAD_DIM]


def _prompt_kernel(sink_ref, x_ref, cos_ref, sin_ref, w_in_ref, w_out_ref, wg_ref, wu_ref, wd_ref,
                   g_pre_mix, g_post_mix, g_pre_ffn, g_post_ffn, g_na, g_nb, g_ret, ln_g, ln_b,
                   gw_ref, gb_ref, abias_ref, d4_ref, xi_ref, zeta_ref, dec_ref, bd_ref,
                   xo_ref, ko_ref, vo_ref, so_ref,
                   y_scr, mix_scr, kw_scr, kws_scr, vw_scr, vws_scr, st_scr):
    tile = x_ref.shape[0]
    blk = PROMPT_BLOCK
    t = pl.program_id(1)

    @pl.when(t == 0)
    def _():
        for ref in (kw_scr, kws_scr, vw_scr, vws_scr):
            ref[0:blk, :] = jnp.zeros((blk, SWA_KV), BF16)
        st_scr[...] = jnp.zeros(st_scr.shape, F32)

    @pl.when(t > 0)
    def _():
        for ref in (kw_scr, kws_scr, vw_scr, vws_scr):
            ref[0:blk, :] = ref[tile:tile + blk, :]

    x = x_ref[...]
    h = _rms(x, g_pre_mix[...]).astype(BF16)
    y_scr[...] = _dot(h, w_in_ref[...])

    k_new = y_scr[:, OFF_KA:OFF_KA + SWA_KV]
    v_new = y_scr[:, OFF_VA:OFF_VA + SWA_KV]
    kw_scr[blk:blk + tile, :] = k_new.astype(BF16)
    kws_scr[blk:blk + tile, :] = pltpu.roll(k_new, HEAD_DIM, 1).astype(BF16)
    vw_scr[blk:blk + tile, :] = v_new.astype(BF16)
    vws_scr[blk:blk + tile, :] = pltpu.roll(v_new, HEAD_DIM, 1).astype(BF16)
    ko_ref[...] = k_new[tile - WINDOW:tile]
    vo_ref[...] = v_new[tile - WINDOW:tile]

    ri = lax.broadcasted_iota(jnp.int32, (blk, blk), 0)
    ci = lax.broadcasted_iota(jnp.int32, (blk, blk), 1)
    keep = jnp.logical_not(jnp.logical_and(ri < CHUNK, ci >= CHUNK))
    w_stack = [jnp.concatenate([jnp.where(keep, gw_ref[2 * j], 0.0), jnp.where(keep, gw_ref[2 * j + 1], 0.0)],
                               axis=0).astype(BF16) for j in range(GMLP_GROUPS // 2)]

    col = lax.broadcasted_iota(jnp.int32, abias_ref.shape, 1)
    first_bias = abias_ref[...] + jnp.where(jnp.logical_and(col < blk, t == 0), NEG, 0.0)

    for j in range(tile // blk):
        r0 = j * blk
        win = pl.ds(r0, 2 * blk)
        bias = first_bias if j == 0 else abias_ref[...]
        state = st_scr[...]

        attn_fn = lambda q: _swa_block(q, kw_scr[win, :], kws_scr[win, :], vw_scr[win, :], vws_scr[win, :],
                                       sink_ref, bias)
        gmlp_fn = lambda u, v: _gmlp_block(u, v, w_stack, gb_ref[...], ln_g[...], ln_b[...])
        ret_fn = lambda q, k, v, g: _retention_block(
            q, k, v, g, cos_ref[pl.ds(r0, blk), :], sin_ref[pl.ds(r0, blk), :], state.astype(BF16),
            d4_ref[...], xi_ref[...], zeta_ref[...], bd_ref[...], g_ret[...])
        _, kv = _mix_rows(y_scr, mix_scr, r0, blk, attn_fn, gmlp_fn, ret_fn, g_na[...], g_nb[...])
        st_scr[...] = state * dec_ref[...] + kv * bd_ref[...]

    _store_diag_blocks(so_ref, st_scr[...])
    xo_ref[...] = _ffn_tail(x, mix_scr, w_out_ref, wg_ref, wu_ref, wd_ref,
                            g_post_mix[...], g_pre_ffn[...], g_post_ffn[...])


def _sample_kernel(sink_ref, x_ref, cos_ref, sin_ref, ck_ref, cv_ref, st0_ref,
                   w_in_ref, w_out_ref, wg_ref, wu_ref, wd_ref,
                   g_pre_mix, g_post_mix, g_pre_ffn, g_post_ffn, g_na, g_nb, g_ret, ln_g, ln_b,
                   gw_ref, gb_ref, d4_ref, xi_ref, zeta_ref, dec_ref, bd_ref,
                   xo_ref, ko_ref, vo_ref, so_ref, gv_ref,
                   y_scr, mix_scr, kw_scr, kws_scr, vw_scr, vws_scr):
    seq = CHUNK
    nseq = x_ref.shape[0] // seq

    x = x_ref[...]
    h = _rms(x, g_pre_mix[...]).astype(BF16)
    y_scr[...] = _dot(h, w_in_ref[...])

    w_stack = [jnp.concatenate([gw_ref[2 * j, 0:seq, 0:seq], gw_ref[2 * j + 1, 0:seq, 0:seq]],
                               axis=0).astype(BF16) for j in range(GMLP_GROUPS // 2)]

    for s in range(nseq):
        r0 = s * seq
        sl = pl.ds(r0, seq)
        k_new = y_scr[sl, OFF_KA:OFF_KA + SWA_KV]
        v_new = y_scr[sl, OFF_VA:OFF_VA + SWA_KV]
        ko_ref[s] = k_new
        vo_ref[s] = v_new
        k_all = jnp.concatenate([ck_ref[s], k_new], axis=0)
        v_all = jnp.concatenate([cv_ref[s], v_new], axis=0)
        kw_scr[...] = k_all.astype(BF16)
        kws_scr[...] = pltpu.roll(k_all, HEAD_DIM, 1).astype(BF16)
        vw_scr[...] = v_all.astype(BF16)
        vws_scr[...] = pltpu.roll(v_all, HEAD_DIM, 1).astype(BF16)

        st0 = st0_ref[s]
        zero = jnp.zeros((HEAD_DIM, HEAD_DIM), F32)
        state = jnp.concatenate(
            [jnp.concatenate([st0[h] if g == h else zero for g in range(RET_HEADS)], axis=1)
             for h in range(RET_HEADS)], axis=0)

        attn_fn = lambda q: _swa_block(q, kw_scr[...], kws_scr[...], vw_scr[...], vws_scr[...], sink_ref, None)
        gmlp_fn = lambda u, v: _gmlp_block(u, v, w_stack, gb_ref[...], ln_g[...], ln_b[...])
        ret_fn = lambda q, k, v, g: _retention_block(
            q, k, v, g, cos_ref[...], sin_ref[...], state.astype(BF16),
            d4_ref[...], xi_ref[...], zeta_ref[...], bd_ref[...], g_ret[...])
        vn, kv = _mix_rows(y_scr, mix_scr, r0, seq, attn_fn, gmlp_fn, ret_fn, g_na[...], g_nb[...])
        gv_ref[s] = vn
        _store_diag_blocks(so_ref.at[s], state * dec_ref[...] + kv)

    xo_ref[...] = _ffn_tail(x, mix_scr, w_out_ref, wg_ref, wu_ref, wd_ref,
                            g_post_mix[...], g_pre_ffn[...], g_post_ffn[...])


def _rotary_tables(pos):
    half = HEAD_DIM // 2
    inv = ROPE_BASE ** (-jnp.arange(half, dtype=F32) / half)
    ang = pos.astype(F32)[:, None] * inv[None, :]
    cos = jnp.cos(ang)
    sin = jnp.sin(ang)
    cos_t = jnp.tile(jnp.concatenate([cos, cos], axis=1), (1, RET_HEADS))
    sin_t = jnp.tile(jnp.concatenate([-sin, sin], axis=1), (1, RET_HEADS))
    return cos_t, sin_t


def _retention_tables(block):
    logg = jnp.log(1.0 - 2.0 ** (-5.0 - jnp.arange(RET_HEADS, dtype=F32)))
    idx = jnp.arange(block, dtype=F32)
    rel = idx[:, None] - idx[None, :]
    decay = jnp.where(rel >= 0, jnp.exp(logg[:, None, None] * jnp.maximum(rel, 0.0)), 0.0)
    d4 = decay.reshape(RET_HEADS * block, block)
    xi = jnp.exp(logg[:, None] * (idx + 1.0))
    zeta = jnp.exp(logg[:, None] * (block - 1.0 - idx))
    xi_tab = jnp.repeat(xi.T, HEAD_DIM, axis=1)
    zeta_tab = jnp.repeat(zeta.T, HEAD_DIM, axis=1)
    chunk_decay = jnp.repeat(jnp.exp(logg * block), HEAD_DIM)
    hid = jnp.arange(RET_WIDTH) // HEAD_DIM
    bd = (hid[:, None] == hid[None, :]).astype(F32)
    dec_tab = bd * chunk_decay[:, None]
    return d4, xi_tab, zeta_tab, dec_tab, bd


def _attn_bias(block):
    r = jnp.arange(block)[:, None]
    c = jnp.arange(2 * block)[None, :]
    visible = jnp.where(r < CHUNK, c < 2 * block - CHUNK, c >= CHUNK)
    return jnp.tile(jnp.where(visible, 0.0, NEG).astype(F32), (4, 1))


def _full(shape):
    return pl.BlockSpec(shape, lambda *_: (0,) * len(shape))


def _layer_weight(shape, layer):
    return pl.BlockSpec((None,) + shape, lambda *_: (layer,) + (0,) * len(shape), pipeline_mode=pl.Buffered(1))


def _const(shape):
    return pl.BlockSpec(shape, lambda *_: (0,) * len(shape), pipeline_mode=pl.Buffered(1))


def _layer_vec(width, layer):
    return pl.BlockSpec((None, 1, width), lambda *_: (layer, 0, 0), pipeline_mode=pl.Buffered(1))


def _weight_specs(layer):
    return [
        _layer_weight((D_MODEL, IN_WIDTH), layer),
        _layer_weight((D_MODEL, D_MODEL), layer),
        _layer_weight((D_MODEL, D_FF), layer),
        _layer_weight((D_MODEL, D_FF), layer),
        _layer_weight((D_FF, D_MODEL), layer),
        _layer_vec(D_MODEL, layer), _layer_vec(D_MODEL, layer), _layer_vec(D_MODEL, layer),
        _layer_vec(D_MODEL, layer),
        _layer_vec(SWA_Q, layer), _layer_vec(GMLP_WIDTH, layer), _layer_vec(RET_WIDTH, layer),
        _layer_vec(GMLP_WIDTH, layer), _layer_vec(GMLP_WIDTH, layer),
        _layer_weight((GMLP_GROUPS, GMLP_BLOCK, GMLP_BLOCK), layer),
    ]


def _prompt_layer(layer, x, sinks, cos_t, sin_t, weights, gb_tab, abias, ret_tabs):
    batch, seq, _ = x.shape
    tile, blk = PROMPT_TILE, PROMPT_BLOCK
    d4, xi_tab, zeta_tab, dec_tab, bd = ret_tabs
    in_specs = [
        pl.BlockSpec(memory_space=pltpu.SMEM),
        pl.BlockSpec((None, tile, D_MODEL), lambda b, t: (b, t, 0)),
        pl.BlockSpec((tile, RET_WIDTH), lambda b, t: (t, 0)),
        pl.BlockSpec((tile, RET_WIDTH), lambda b, t: (t, 0)),
        *_weight_specs(layer),
        pl.BlockSpec((None, blk, GMLP_WIDTH), lambda b, t: (layer, 0, 0), pipeline_mode=pl.Buffered(1)),
        _const(abias.shape), _const(d4.shape), _const(xi_tab.shape), _const(zeta_tab.shape),
        _const(dec_tab.shape), _const(bd.shape),
    ]
    out_shape = [
        jax.ShapeDtypeStruct((batch, seq, D_MODEL), F32),
        jax.ShapeDtypeStruct((batch, WINDOW, SWA_KV), F32),
        jax.ShapeDtypeStruct((batch, WINDOW, SWA_KV), F32),
        jax.ShapeDtypeStruct((batch, RET_HEADS, HEAD_DIM, HEAD_DIM), F32),
    ]
    out_specs = [
        pl.BlockSpec((None, tile, D_MODEL), lambda b, t: (b, t, 0)),
        pl.BlockSpec((None, WINDOW, SWA_KV), lambda b, t: (b, 0, 0)),
        pl.BlockSpec((None, WINDOW, SWA_KV), lambda b, t: (b, 0, 0)),
        pl.BlockSpec((None, RET_HEADS, HEAD_DIM, HEAD_DIM), lambda b, t: (b, 0, 0, 0)),
    ]
    scratch = [
        pltpu.VMEM((tile, IN_WIDTH), F32),
        pltpu.VMEM((tile, D_MODEL), BF16),
        pltpu.VMEM((tile + blk, SWA_KV), BF16), pltpu.VMEM((tile + blk, SWA_KV), BF16),
        pltpu.VMEM((tile + blk, SWA_KV), BF16), pltpu.VMEM((tile + blk, SWA_KV), BF16),
        pltpu.VMEM((RET_WIDTH, RET_WIDTH), F32),
    ]
    return pl.pallas_call(
        _prompt_kernel,
        grid=(batch, seq // tile),
        in_specs=in_specs, out_specs=out_specs, out_shape=out_shape, scratch_shapes=scratch,
        compiler_params=pltpu.CompilerParams(dimension_semantics=("arbitrary", "arbitrary"),
                                             vmem_limit_bytes=VMEM_LIMIT_BYTES),
        name=f"prompt_layer{layer}",
    )(sinks, x, cos_t, sin_t, *weights, gb_tab, abias, d4, xi_tab, zeta_tab, dec_tab, bd)


def _sample_layer(layer, x, sinks, cos_t, sin_t, cache_k, cache_v, state0, weights, gb_tab, ret_tabs):
    rows = x.shape[0]
    nseq = SAMPLE_SEQS
    tile = nseq * CHUNK
    n_all = rows // CHUNK
    d4, xi_tab, zeta_tab, dec_tab, bd = ret_tabs
    in_specs = [
        pl.BlockSpec(memory_space=pltpu.SMEM),
        pl.BlockSpec((tile, D_MODEL), lambda i: (i, 0)),
        _const(cos_t.shape), _const(sin_t.shape),
        pl.BlockSpec((nseq, None, WINDOW, SWA_KV), lambda i: (i, layer, 0, 0)),
        pl.BlockSpec((nseq, None, WINDOW, SWA_KV), lambda i: (i, layer, 0, 0)),
        pl.BlockSpec((nseq, None, RET_HEADS, HEAD_DIM, HEAD_DIM), lambda i: (i, layer, 0, 0, 0)),
        *_weight_specs(layer),
        pl.BlockSpec((None, CHUNK, GMLP_WIDTH), lambda i: (layer, 0, 0), pipeline_mode=pl.Buffered(1)),
        _const(d4.shape), _const(xi_tab.shape), _const(zeta_tab.shape), _const(dec_tab.shape), _const(bd.shape),
    ]
    out_shape = [
        jax.ShapeDtypeStruct((rows, D_MODEL), F32),
        jax.ShapeDtypeStruct((n_all, CHUNK, SWA_KV), F32),
        jax.ShapeDtypeStruct((n_all, CHUNK, SWA_KV), F32),
        jax.ShapeDtypeStruct((n_all, RET_HEADS, HEAD_DIM, HEAD_DIM), F32),
        jax.ShapeDtypeStruct((n_all, CHUNK, GMLP_WIDTH), F32),
    ]
    out_specs = [
        pl.BlockSpec((tile, D_MODEL), lambda i: (i, 0)),
        pl.BlockSpec((nseq, CHUNK, SWA_KV), lambda i: (i, 0, 0)),
        pl.BlockSpec((nseq, CHUNK, SWA_KV), lambda i: (i, 0, 0)),
        pl.BlockSpec((nseq, RET_HEADS, HEAD_DIM, HEAD_DIM), lambda i: (i, 0, 0, 0)),
        pl.BlockSpec((nseq, CHUNK, GMLP_WIDTH), lambda i: (i, 0, 0)),
    ]
    win = WINDOW + CHUNK
    scratch = [
        pltpu.VMEM((tile, IN_WIDTH), F32),
        pltpu.VMEM((tile, D_MODEL), BF16),
        pltpu.VMEM((win, SWA_KV), BF16), pltpu.VMEM((win, SWA_KV), BF16),
        pltpu.VMEM((win, SWA_KV), BF16), pltpu.VMEM((win, SWA_KV), BF16),
    ]
    return pl.pallas_call(
        _sample_kernel,
        grid=(n_all // nseq,),
        in_specs=in_specs, out_specs=out_specs, out_shape=out_shape, scratch_shapes=scratch,
        compiler_params=pltpu.CompilerParams(dimension_semantics=("arbitrary",),
                                             vmem_limit_bytes=VMEM_LIMIT_BYTES),
        name=f"sample_layer{layer}",
    )(sinks, x, cos_t, sin_t, cache_k, cache_v, state0, *weights, gb_tab, d4, xi_tab, zeta_tab, dec_tab, bd)


def kernel(x_prompt, x_sample, cache_swa_k, cache_swa_v, state_ret, w_in, w_out, swa_sinks,
           gmlp_w, gmlp_b, gmlp_ln_g, gmlp_ln_b, norm_a_g, norm_b_g, ret_norm_g,
           ln_pre_mix, ln_post_mix, ln_pre_ffn, ln_post_ffn, w_gate, w_up, w_down):
    batch, seq, _ = x_prompt.shape
    dec_batch, dec_seq, _ = x_sample.shape
    assert dec_seq == CHUNK and seq % PROMPT_TILE == 0 and dec_batch % SAMPLE_SEQS == 0
    assert cache_swa_k.shape[2] == WINDOW

    vec = lambda a: a.reshape(DEPTH, 1, a.shape[-1])
    weights = [
        w_in.astype(BF16), w_out.astype(BF16), w_gate.astype(BF16), w_up.astype(BF16), w_down.astype(BF16),
        vec(ln_pre_mix), vec(ln_post_mix), vec(ln_pre_ffn), vec(ln_post_ffn),
        vec(norm_a_g), vec(norm_b_g), vec(ret_norm_g), vec(gmlp_ln_g), vec(gmlp_ln_b),
        gmlp_w,
    ]
    gb_tab = jnp.repeat(jnp.swapaxes(gmlp_b, 1, 2), HEAD_DIM, axis=2)

    cos_p, sin_p = _rotary_tables(jnp.arange(seq, dtype=jnp.int32))
    cos_s, sin_s = _rotary_tables(PAST_LEN + jnp.arange(dec_seq, dtype=jnp.int32))
    tabs_p = _retention_tables(PROMPT_BLOCK)
    tabs_s = _retention_tables(CHUNK)
    abias = _attn_bias(PROMPT_BLOCK)

    xs = x_sample.reshape(dec_batch * dec_seq, D_MODEL)
    ck = cache_swa_k.reshape(dec_batch, DEPTH, WINDOW, SWA_KV)
    cv = cache_swa_v.reshape(dec_batch, DEPTH, WINDOW, SWA_KV)

    xp = x_prompt
    kp_l, vp_l, ks_l, vs_l, rp_l, rs_l, gv_l = [], [], [], [], [], [], []
    for layer in range(DEPTH):
        sinks = swa_sinks[layer]
        xp, kp, vp, rp = _prompt_layer(layer, xp, sinks, cos_p, sin_p, weights, gb_tab, abias, tabs_p)
        xs, ks, vs, rs, gv = _sample_layer(layer, xs, sinks, cos_s, sin_s, ck, cv, state_ret, weights,
                                           gb_tab, tabs_s)
        kp_l.append(kp); vp_l.append(vp); ks_l.append(ks); vs_l.append(vs)
        rp_l.append(rp); rs_l.append(rs); gv_l.append(gv)

    kv5 = lambda a: jnp.stack(a, axis=1).reshape(a[0].shape[0], DEPTH, a[0].shape[1], SWA_KV_HEADS, HEAD_DIM)
    return (xp, xs.reshape(dec_batch, dec_seq, D_MODEL), kv5(kp_l), kv5(vp_l), kv5(ks_l), kv5(vs_l),
            jnp.stack(rp_l, axis=1), jnp.stack(rs_l, axis=1), jnp.stack(gv_l, axis=1))
```

```python
import functools
import math

import numpy as np
import jax
import jax.numpy as jnp
from jax import lax
from jax.experimental import pallas as pl
from jax.experimental.pallas import tpu as pltpu

D_MODEL = 1024
DEPTH = 4
PAST_LEN = 2048
CHUNK = 64
HEAD_DIM = 64
SWA_HEADS = 8
SWA_KV_HEADS = 2
WINDOW = 128
SWA_Q = SWA_HEADS * HEAD_DIM
SWA_KV = SWA_KV_HEADS * HEAD_DIM
GMLP_GROUPS = 4
GMLP_BLOCK = 128
GMLP_WIDTH = GMLP_GROUPS * HEAD_DIM
RET_HEADS = 4
RET_WIDTH = RET_HEADS * HEAD_DIM
ROPE_BASE = 10000.0
IN_WIDTH = SWA_Q + 2 * SWA_KV + 2 * GMLP_WIDTH + 4 * RET_WIDTH
D_FF = 2816
EPS = 1e-6
NEG = -1e30

OFF_QA, OFF_KA, OFF_VA, OFF_UB, OFF_VB, OFF_QC, OFF_KC, OFF_VC, OFF_GC = (
    0, 512, 640, 768, 1024, 1280, 1536, 1792, 2048)

LANES = 128
PROMPT_BLOCK = 128
PROMPT_TILE = 256
SAMPLE_SEQS = 4
VMEM_LIMIT_BYTES = 56 * 1024 * 1024

HEADS_PLAIN = (0, 2, 5, 7)
HEADS_SWAPPED = (1, 3, 4, 6)

F32 = jnp.float32
BF16 = jnp.bfloat16


def _dot(a, b):
    return jnp.dot(a, b, preferred_element_type=F32)


def _dot_nt(a, b):
    return lax.dot_general(a, b, (((1,), (1,)), ((), ())), preferred_element_type=F32)


def _rms(x, g):
    return x * lax.rsqrt(jnp.mean(x * x, axis=-1, keepdims=True) + EPS) * g


def _gelu(x):
    c = math.sqrt(2.0 / math.pi)
    return x * (0.5 * (1.0 + jnp.tanh(c * (x + 0.044715 * (x * x * x)))))


def _silu(x):
    return x / (1.0 + jnp.exp(-x))


def _swa_block(q, kw, kw_sw, vw, vw_sw, sink_ref, bias):
    rows = q.shape[0]
    lo = lax.broadcasted_iota(jnp.int32, (rows, LANES), 1) < HEAD_DIM
    scale = HEAD_DIM ** -0.5

    def half(pair, keep_lo):
        qp = q[:, pair * LANES:(pair + 1) * LANES] * scale
        return jnp.where(lo if keep_lo else jnp.logical_not(lo), qp, 0.0).astype(BF16)

    lhs_plain = jnp.concatenate([half(0, True), half(1, True), half(2, False), half(3, False)], axis=0)
    lhs_swap = jnp.concatenate([half(0, False), half(1, False), half(2, True), half(3, True)], axis=0)

    def attend(lhs, k, v, heads):
        s = _dot_nt(lhs, k)
        if bias is not None:
            s = s + bias
        ps, dens = [], []
        for g, h in enumerate(heads):
            sh = s[g * rows:(g + 1) * rows]
            m = jnp.maximum(jnp.max(sh, axis=-1, keepdims=True), sink_ref[h])
            p = jnp.exp(sh - m)
            dens.append(jnp.sum(p, axis=-1, keepdims=True) + jnp.exp(sink_ref[h] - m))
            ps.append(p.astype(BF16))
        o = _dot(jnp.concatenate(ps, axis=0), v)
        return jnp.concatenate([o[g * rows:(g + 1) * rows] / dens[g] for g in range(len(heads))], axis=0)

    o_p = attend(lhs_plain, kw, vw, HEADS_PLAIN)
    o_s = attend(lhs_swap, kw_sw, vw_sw, HEADS_SWAPPED)
    r = rows
    pairs = [
        jnp.where(lo, o_p[0:r], o_s[0:r]),
        jnp.where(lo, o_p[r:2 * r], o_s[r:2 * r]),
        jnp.where(lo, o_s[2 * r:3 * r], o_p[2 * r:3 * r]),
        jnp.where(lo, o_s[3 * r:4 * r], o_p[3 * r:4 * r]),
    ]
    return jnp.concatenate(pairs, axis=1)


def _gmlp_block(u_raw, v_raw, w_stack, bias_tab, ln_g, ln_b):
    rows = u_raw.shape[0]
    u = _gelu(u_raw)
    v = _gelu(v_raw)
    mu = jnp.mean(v, axis=-1, keepdims=True)
    d = v - mu
    var = jnp.mean(d * d, axis=-1, keepdims=True)
    vn = d * lax.rsqrt(var + EPS) * ln_g + ln_b
    lo = lax.broadcasted_iota(jnp.int32, (rows, LANES), 1) < HEAD_DIM
    vb = vn.astype(BF16)
    mixed = []
    for j in range(GMLP_GROUPS // 2):
        res = _dot(w_stack[j], vb[:, j * LANES:(j + 1) * LANES])
        mixed.append(jnp.where(lo, res[0:rows], res[rows:2 * rows]))
    mixed = jnp.concatenate(mixed, axis=1) + bias_tab
    return u * mixed, vn


def _rotary(x, cos_t, sin_t):
    rows = x.shape[0]
    first = (lax.broadcasted_iota(jnp.int32, (rows, LANES), 1) % HEAD_DIM) < (HEAD_DIM // 2)
    out = []
    for j in range(RET_WIDTH // LANES):
        sl = slice(j * LANES, (j + 1) * LANES)
        xh = x[:, sl]
        partner = jnp.where(first, pltpu.roll(xh, LANES - HEAD_DIM // 2, 1), pltpu.roll(xh, HEAD_DIM // 2, 1))
        out.append(xh * cos_t[:, sl] + partner * sin_t[:, sl])
    return jnp.concatenate(out, axis=1)


def _retention_block(qc, kc, vc, gc, cos_t, sin_t, state_bf, d4, xi_tab, zeta_tab, bd_mask, ret_g):
    rows = qc.shape[0]
    qr = _rotary(qc, cos_t, sin_t)
    kr = _rotary(kc, cos_t, sin_t) * (HEAD_DIM ** -0.5)
    head = lax.broadcasted_iota(jnp.int32, (rows, RET_WIDTH), 1) // HEAD_DIM
    q4 = jnp.concatenate([jnp.where(head == h, qr, 0.0) for h in range(RET_HEADS)], axis=0).astype(BF16)
    vb = vc.astype(BF16)
    s4 = _dot_nt(q4, kr.astype(BF16)) * d4
    o4 = _dot(s4.astype(BF16), vb)
    inner = o4[0:rows]
    for h in range(1, RET_HEADS):
        inner = jnp.where(head == h, o4[h * rows:(h + 1) * rows], inner)
    cross = _dot(qr.astype(BF16), state_bf) * xi_tab
    r = inner + cross
    kz_t = jnp.transpose(kr * zeta_tab).astype(BF16)
    kv = _dot(kz_t, vb)
    avg = (bd_mask * (1.0 / HEAD_DIM)).astype(BF16)
    mu = _dot(r.astype(BF16), avg)
    d = r - mu
    var = _dot((d * d).astype(BF16), avg)
    yn = d * lax.rsqrt(var + EPS) * ret_g
    return yn * _silu(gc), kv


def _mix_rows(y_ref, mix_ref, r0, rows, attn_fn, gmlp_fn, ret_fn, g_na, g_nb):
    sl = pl.ds(r0, rows)
    a = attn_fn(y_ref[sl, OFF_QA:OFF_QA + SWA_Q])
    mix_ref[sl, 0:SWA_Q] = _rms(a, g_na).astype(BF16)
    b, vn = gmlp_fn(y_ref[sl, OFF_UB:OFF_UB + GMLP_WIDTH], y_ref[sl, OFF_VB:OFF_VB + GMLP_WIDTH])
    mix_ref[sl, SWA_Q:SWA_Q + GMLP_WIDTH] = _rms(b, g_nb).astype(BF16)
    c, kv = ret_fn(y_ref[sl, OFF_QC:OFF_QC + RET_WIDTH], y_ref[sl, OFF_KC:OFF_KC + RET_WIDTH],
                   y_ref[sl, OFF_VC:OFF_VC + RET_WIDTH], y_ref[sl, OFF_GC:OFF_GC + RET_WIDTH])
    mix_ref[sl, SWA_Q + GMLP_WIDTH:D_MODEL] = c.astype(BF16)
    return vn, kv


def _ffn_tail(x, mix_ref, w_out_ref, wg_ref, wu_ref, wd_ref, g_post_mix, g_pre_ffn, g_post_ffn):
    merged = _dot(mix_ref[...], w_out_ref[...])
    x1 = x + _rms(merged, g_post_mix)
    h2 = _rms(x1, g_pre_ffn).astype(BF16)
    gate = _dot(h2, wg_ref[...])
    up = _dot(h2, wu_ref[...])
    act = (_silu(gate) * up).astype(BF16)
    f = _dot(act, wd_ref[...])
    return x1 + _rms(f, g_post_ffn)


def _store_diag_blocks(out_ref, m):
    for h in range(RET_HEADS):
        out_ref[h] = m[h * HEAD_DIM:(h + 1) * HEAD_DIM, h * HEAD_DIM:(h + 1) * HEAD_DIM]


def _prompt_kernel(tiles_per_seq, n_tiles,
                   sink_ref, x_ref, xprev_ref, cos_ref, sin_ref, w_in_ref, w_out_ref, wg_ref, wu_ref, wd_ref,
                   g_pre_mix, g_post_mix, g_pre_ffn, g_post_ffn, g_na, g_nb, g_ret, ln_g, ln_b,
                   gw_ref, gb_ref, abias_ref, d4_ref, xi_ref, zeta_ref, dec_ref, bd_ref,
                   xo_ref, ko_ref, vo_ref, so_ref,
                   y_scr, mix_scr, kw_scr, kws_scr, vw_scr, vws_scr, st_scr):
    tile = x_ref.shape[0]
    blk = PROMPT_BLOCK
    i = pl.program_id(0)
    t = jnp.minimum(i, n_tiles - 1) % tiles_per_seq
    live = i < n_tiles

    @pl.when(i == 0)
    def _():
        mix_scr[...] = jnp.zeros(mix_scr.shape, BF16)

    @pl.when(t == 0)
    def _():
        for ref in (kw_scr, kws_scr, vw_scr, vws_scr):
            ref[0:blk, :] = jnp.zeros((blk, SWA_KV), BF16)
        st_scr[...] = jnp.zeros(st_scr.shape, F32)

    @pl.when(jnp.logical_and(t > 0, live))
    def _():
        for ref in (kw_scr, kws_scr, vw_scr, vws_scr):
            ref[0:blk, :] = ref[tile:tile + blk, :]

    xo_ref[...] = _ffn_tail(xprev_ref[...], mix_scr, w_out_ref, wg_ref, wu_ref, wd_ref,
                            g_post_mix[...], g_pre_ffn[...], g_post_ffn[...])

    h = _rms(x_ref[...], g_pre_mix[...]).astype(BF16)
    y_scr[...] = _dot(h, w_in_ref[...])

    k_new = y_scr[:, OFF_KA:OFF_KA + SWA_KV]
    v_new = y_scr[:, OFF_VA:OFF_VA + SWA_KV]
    kw_scr[blk:blk + tile, :] = k_new.astype(BF16)
    kws_scr[blk:blk + tile, :] = pltpu.roll(k_new, HEAD_DIM, 1).astype(BF16)
    vw_scr[blk:blk + tile, :] = v_new.astype(BF16)
    vws_scr[blk:blk + tile, :] = pltpu.roll(v_new, HEAD_DIM, 1).astype(BF16)
    ko_ref[...] = k_new[tile - WINDOW:tile]
    vo_ref[...] = v_new[tile - WINDOW:tile]

    ri = lax.broadcasted_iota(jnp.int32, (blk, blk), 0)
    ci = lax.broadcasted_iota(jnp.int32, (blk, blk), 1)
    keep = jnp.logical_not(jnp.logical_and(ri < CHUNK, ci >= CHUNK))
    w_stack = [jnp.concatenate([jnp.where(keep, gw_ref[2 * j], 0.0), jnp.where(keep, gw_ref[2 * j + 1], 0.0)],
                               axis=0).astype(BF16) for j in range(GMLP_GROUPS // 2)]

    col = lax.broadcasted_iota(jnp.int32, abias_ref.shape, 1)
    first_bias = abias_ref[...] + jnp.where(jnp.logical_and(col < blk, t == 0), NEG, 0.0)

    for j in range(tile // blk):
        r0 = j * blk
        win = pl.ds(r0, 2 * blk)
        bias = first_bias if j == 0 else abias_ref[...]
        state = st_scr[...]

        attn_fn = lambda q: _swa_block(q, kw_scr[win, :], kws_scr[win, :], vw_scr[win, :], vws_scr[win, :],
                                       sink_ref, bias)
        gmlp_fn = lambda u, v: _gmlp_block(u, v, w_stack, gb_ref[...], ln_g[...], ln_b[...])
        ret_fn = lambda q, k, v, g: _retention_block(
            q, k, v, g, cos_ref[pl.ds(r0, blk), :], sin_ref[pl.ds(r0, blk), :], state.astype(BF16),
            d4_ref[...], xi_ref[...], zeta_ref[...], bd_ref[...], g_ret[...])
        _, kv = _mix_rows(y_scr, mix_scr, r0, blk, attn_fn, gmlp_fn, ret_fn, g_na[...], g_nb[...])
        st_scr[...] = jnp.where(live, state * dec_ref[...] + kv * bd_ref[...], state)

    _store_diag_blocks(so_ref, st_scr[...])


def _sample_kernel(sink_ref, x_ref, cos_ref, sin_ref, ck_ref, cv_ref, st0_ref,
                   w_in_ref, w_out_ref, wg_ref, wu_ref, wd_ref,
                   g_pre_mix, g_post_mix, g_pre_ffn, g_post_ffn, g_na, g_nb, g_ret, ln_g, ln_b,
                   gw_ref, gb_ref, d4_ref, xi_ref, zeta_ref, dec_ref, bd_ref,
                   xo_ref, ko_ref, vo_ref, so_ref, gv_ref,
                   y_scr, mix_scr, kw_scr, kws_scr, vw_scr, vws_scr):
    seq = CHUNK
    nseq = x_ref.shape[0] // seq

    x = x_ref[...]
    h = _rms(x, g_pre_mix[...]).astype(BF16)
    y_scr[...] = _dot(h, w_in_ref[...])

    w_stack = [jnp.concatenate([gw_ref[2 * j, 0:seq, 0:seq], gw_ref[2 * j + 1, 0:seq, 0:seq]],
                               axis=0).astype(BF16) for j in range(GMLP_GROUPS // 2)]

    for s in range(nseq):
        r0 = s * seq
        sl = pl.ds(r0, seq)
        k_new = y_scr[sl, OFF_KA:OFF_KA + SWA_KV]
        v_new = y_scr[sl, OFF_VA:OFF_VA + SWA_KV]
        ko_ref[s] = k_new
        vo_ref[s] = v_new
        k_all = jnp.concatenate([ck_ref[s], k_new], axis=0)
        v_all = jnp.concatenate([cv_ref[s], v_new], axis=0)
        kw_scr[...] = k_all.astype(BF16)
        kws_scr[...] = pltpu.roll(k_all, HEAD_DIM, 1).astype(BF16)
        vw_scr[...] = v_all.astype(BF16)
        vws_scr[...] = pltpu.roll(v_all, HEAD_DIM, 1).astype(BF16)

        st0 = st0_ref[s]
        zero = jnp.zeros((HEAD_DIM, HEAD_DIM), F32)
        state = jnp.concatenate(
            [jnp.concatenate([st0[h] if g == h else zero for g in range(RET_HEADS)], axis=1)
             for h in range(RET_HEADS)], axis=0)

        attn_fn = lambda q: _swa_block(q, kw_scr[...], kws_scr[...], vw_scr[...], vws_scr[...], sink_ref, None)
        gmlp_fn = lambda u, v: _gmlp_block(u, v, w_stack, gb_ref[...], ln_g[...], ln_b[...])
        ret_fn = lambda q, k, v, g: _retention_block(
            q, k, v, g, cos_ref[...], sin_ref[...], state.astype(BF16),
            d4_ref[...], xi_ref[...], zeta_ref[...], bd_ref[...], g_ret[...])
        vn, kv = _mix_rows(y_scr, mix_scr, r0, seq, attn_fn, gmlp_fn, ret_fn, g_na[...], g_nb[...])
        gv_ref[s] = vn
        _store_diag_blocks(so_ref.at[s], state * dec_ref[...] + kv)

    xo_ref[...] = _ffn_tail(x, mix_scr, w_out_ref, wg_ref, wu_ref, wd_ref,
                            g_post_mix[...], g_pre_ffn[...], g_post_ffn[...])


def _rotary_tables(pos):
    half = HEAD_DIM // 2
    inv = ROPE_BASE ** (-jnp.arange(half, dtype=F32) / half)
    ang = pos.astype(F32)[:, None] * inv[None, :]
    cos = jnp.cos(ang)
    sin = jnp.sin(ang)
    cos_t = jnp.tile(jnp.concatenate([cos, cos], axis=1), (1, RET_HEADS))
    sin_t = jnp.tile(jnp.concatenate([-sin, sin], axis=1), (1, RET_HEADS))
    return cos_t, sin_t


def _retention_tables(block):
    logg = jnp.log(1.0 - 2.0 ** (-5.0 - jnp.arange(RET_HEADS, dtype=F32)))
    idx = jnp.arange(block, dtype=F32)
    rel = idx[:, None] - idx[None, :]
    decay = jnp.where(rel >= 0, jnp.exp(logg[:, None, None] * jnp.maximum(rel, 0.0)), 0.0)
    d4 = decay.reshape(RET_HEADS * block, block)
    xi = jnp.exp(logg[:, None] * (idx + 1.0))
    zeta = jnp.exp(logg[:, None] * (block - 1.0 - idx))
    xi_tab = jnp.repeat(xi.T, HEAD_DIM, axis=1)
    zeta_tab = jnp.repeat(zeta.T, HEAD_DIM, axis=1)
    chunk_decay = jnp.repeat(jnp.exp(logg * block), HEAD_DIM)
    hid = jnp.arange(RET_WIDTH) // HEAD_DIM
    bd = (hid[:, None] == hid[None, :]).astype(F32)
    dec_tab = bd * chunk_decay[:, None]
    return d4, xi_tab, zeta_tab, dec_tab, bd


def _attn_bias(block):
    r = jnp.arange(block)[:, None]
    c = jnp.arange(2 * block)[None, :]
    visible = jnp.where(r < CHUNK, c < 2 * block - CHUNK, c >= CHUNK)
    return jnp.tile(jnp.where(visible, 0.0, NEG).astype(F32), (4, 1))


def _full(shape):
    return pl.BlockSpec(shape, lambda *_: (0,) * len(shape))


def _layer_weight(shape, layer):
    return pl.BlockSpec((None,) + shape, lambda *_: (layer,) + (0,) * len(shape), pipeline_mode=pl.Buffered(1))


def _const(shape):
    return pl.BlockSpec(shape, lambda *_: (0,) * len(shape), pipeline_mode=pl.Buffered(1))


def _layer_vec(width, layer):
    return pl.BlockSpec((None, 1, width), lambda *_: (layer, 0, 0), pipeline_mode=pl.Buffered(1))


def _weight_specs(layer):
    return [
        _layer_weight((D_MODEL, IN_WIDTH), layer),
        _layer_weight((D_MODEL, D_MODEL), layer),
        _layer_weight((D_MODEL, D_FF), layer),
        _layer_weight((D_MODEL, D_FF), layer),
        _layer_weight((D_FF, D_MODEL), layer),
        _layer_vec(D_MODEL, layer), _layer_vec(D_MODEL, layer), _layer_vec(D_MODEL, layer),
        _layer_vec(D_MODEL, layer),
        _layer_vec(SWA_Q, layer), _layer_vec(GMLP_WIDTH, layer), _layer_vec(RET_WIDTH, layer),
        _layer_vec(GMLP_WIDTH, layer), _layer_vec(GMLP_WIDTH, layer),
        _layer_weight((GMLP_GROUPS, GMLP_BLOCK, GMLP_BLOCK), layer),
    ]


def _prompt_layer(layer, x, sinks, cos_t, sin_t, weights, gb_tab, abias, ret_tabs):
    batch, seq, _ = x.shape
    tile, blk = PROMPT_TILE, PROMPT_BLOCK
    d4, xi_tab, zeta_tab, dec_tab, bd = ret_tabs
    tiles_per_seq = seq // tile
    n_tiles = batch * tiles_per_seq
    cur = lambda i: jnp.minimum(i, n_tiles - 1)
    prev = lambda i: jnp.maximum(i - 1, 0)
    in_specs = [
        pl.BlockSpec(memory_space=pltpu.SMEM),
        pl.BlockSpec((None, tile, D_MODEL), lambda i: (cur(i) // tiles_per_seq, cur(i) % tiles_per_seq, 0)),
        pl.BlockSpec((None, tile, D_MODEL), lambda i: (prev(i) // tiles_per_seq, prev(i) % tiles_per_seq, 0)),
        pl.BlockSpec((tile, RET_WIDTH), lambda i: (cur(i) % tiles_per_seq, 0)),
        pl.BlockSpec((tile, RET_WIDTH), lambda i: (cur(i) % tiles_per_seq, 0)),
        *_weight_specs(layer),
        pl.BlockSpec((None, blk, GMLP_WIDTH), lambda i: (layer, 0, 0), pipeline_mode=pl.Buffered(1)),
        _const(abias.shape), _const(d4.shape), _const(xi_tab.shape), _const(zeta_tab.shape),
        _const(dec_tab.shape), _const(bd.shape),
    ]
    out_shape = [
        jax.ShapeDtypeStruct((batch, seq, D_MODEL), F32),
        jax.ShapeDtypeStruct((batch, WINDOW, SWA_KV), F32),
        jax.ShapeDtypeStruct((batch, WINDOW, SWA_KV), F32),
        jax.ShapeDtypeStruct((batch, RET_HEADS, HEAD_DIM, HEAD_DIM), F32),
    ]
    out_specs = [
        pl.BlockSpec((None, tile, D_MODEL), lambda i: (prev(i) // tiles_per_seq, prev(i) % tiles_per_seq, 0)),
        pl.BlockSpec((None, WINDOW, SWA_KV), lambda i: (cur(i) // tiles_per_seq, 0, 0)),
        pl.BlockSpec((None, WINDOW, SWA_KV), lambda i: (cur(i) // tiles_per_seq, 0, 0)),
        pl.BlockSpec((None, RET_HEADS, HEAD_DIM, HEAD_DIM), lambda i: (cur(i) // tiles_per_seq, 0, 0, 0)),
    ]
    scratch = [
        pltpu.VMEM((tile, IN_WIDTH), F32),
        pltpu.VMEM((tile, D_MODEL), BF16),
        pltpu.VMEM((tile + blk, SWA_KV), BF16), pltpu.VMEM((tile + blk, SWA_KV), BF16),
        pltpu.VMEM((tile + blk, SWA_KV), BF16), pltpu.VMEM((tile + blk, SWA_KV), BF16),
        pltpu.VMEM((RET_WIDTH, RET_WIDTH), F32),
    ]
    return pl.pallas_call(
        functools.partial(_prompt_kernel, tiles_per_seq, n_tiles),
        grid=(n_tiles + 1,),
        in_specs=in_specs, out_specs=out_specs, out_shape=out_shape, scratch_shapes=scratch,
        compiler_params=pltpu.CompilerParams(dimension_semantics=("arbitrary",),
                                             vmem_limit_bytes=VMEM_LIMIT_BYTES),
        name=f"prompt_layer{layer}",
    )(sinks, x, x, cos_t, sin_t, *weights, gb_tab, abias, d4, xi_tab, zeta_tab, dec_tab, bd)


def _sample_layer(layer, x, sinks, cos_t, sin_t, cache_k, cache_v, state0, weights, gb_tab, ret_tabs):
    rows = x.shape[0]
    nseq = SAMPLE_SEQS
    tile = nseq * CHUNK
    n_all = rows // CHUNK
    d4, xi_tab, zeta_tab, dec_tab, bd = ret_tabs
    in_specs = [
        pl.BlockSpec(memory_space=pltpu.SMEM),
        pl.BlockSpec((tile, D_MODEL), lambda i: (i, 0)),
        _const(cos_t.shape), _const(sin_t.shape),
        pl.BlockSpec((nseq, None, WINDOW, SWA_KV), lambda i: (i, layer, 0, 0)),
        pl.BlockSpec((nseq, None, WINDOW, SWA_KV), lambda i: (i, layer, 0, 0)),
        pl.BlockSpec((nseq, None, RET_HEADS, HEAD_DIM, HEAD_DIM), lambda i: (i, layer, 0, 0, 0)),
        *_weight_specs(layer),
        pl.BlockSpec((None, CHUNK, GMLP_WIDTH), lambda i: (layer, 0, 0), pipeline_mode=pl.Buffered(1)),
        _const(d4.shape), _const(xi_tab.shape), _const(zeta_tab.shape), _const(dec_tab.shape), _const(bd.shape),
    ]
    out_shape = [
        jax.ShapeDtypeStruct((rows, D_MODEL), F32),
        jax.ShapeDtypeStruct((n_all, CHUNK, SWA_KV), F32),
        jax.ShapeDtypeStruct((n_all, CHUNK, SWA_KV), F32),
        jax.ShapeDtypeStruct((n_all, RET_HEADS, HEAD_DIM, HEAD_DIM), F32),
        jax.ShapeDtypeStruct((n_all, CHUNK, GMLP_WIDTH), F32),
    ]
    out_specs = [
        pl.BlockSpec((tile, D_MODEL), lambda i: (i, 0)),
        pl.BlockSpec((nseq, CHUNK, SWA_KV), lambda i: (i, 0, 0)),
        pl.BlockSpec((nseq, CHUNK, SWA_KV), lambda i: (i, 0, 0)),
        pl.BlockSpec((nseq, RET_HEADS, HEAD_DIM, HEAD_DIM), lambda i: (i, 0, 0, 0)),
        pl.BlockSpec((nseq, CHUNK, GMLP_WIDTH), lambda i: (i, 0, 0)),
    ]
    win = WINDOW + CHUNK
    scratch = [
        pltpu.VMEM((tile, IN_WIDTH), F32),
        pltpu.VMEM((tile, D_MODEL), BF16),
        pltpu.VMEM((win, SWA_KV), BF16), pltpu.VMEM((win, SWA_KV), BF16),
        pltpu.VMEM((win, SWA_KV), BF16), pltpu.VMEM((win, SWA_KV), BF16),
    ]
    return pl.pallas_call(
        _sample_kernel,
        grid=(n_all // nseq,),
        in_specs=in_specs, out_specs=out_specs, out_shape=out_shape, scratch_shapes=scratch,
        compiler_params=pltpu.CompilerParams(dimension_semantics=("arbitrary",),
                                             vmem_limit_bytes=VMEM_LIMIT_BYTES),
        name=f"sample_layer{layer}",
    )(sinks, x, cos_t, sin_t, cache_k, cache_v, state0, *weights, gb_tab, d4, xi_tab, zeta_tab, dec_tab, bd)


def kernel(x_prompt, x_sample, cache_swa_k, cache_swa_v, state_ret, w_in, w_out, swa_sinks,
           gmlp_w, gmlp_b, gmlp_ln_g, gmlp_ln_b, norm_a_g, norm_b_g, ret_norm_g,
           ln_pre_mix, ln_post_mix, ln_pre_ffn, ln_post_ffn, w_gate, w_up, w_down):
    batch, seq, _ = x_prompt.shape
    dec_batch, dec_seq, _ = x_sample.shape
    assert dec_seq == CHUNK and seq % PROMPT_TILE == 0 and dec_batch % SAMPLE_SEQS == 0
    assert cache_swa_k.shape[2] == WINDOW

    vec = lambda a: a.reshape(DEPTH, 1, a.shape[-1])
    weights = [
        w_in.astype(BF16), w_out.astype(BF16), w_gate.astype(BF16), w_up.astype(BF16), w_down.astype(BF16),
        vec(ln_pre_mix), vec(ln_post_mix), vec(ln_pre_ffn), vec(ln_post_ffn),
        vec(norm_a_g), vec(norm_b_g), vec(ret_norm_g), vec(gmlp_ln_g), vec(gmlp_ln_b),
        gmlp_w,
    ]
    gb_tab = jnp.repeat(jnp.swapaxes(gmlp_b, 1, 2), HEAD_DIM, axis=2)

    cos_p, sin_p = _rotary_tables(jnp.arange(seq, dtype=jnp.int32))
    cos_s, sin_s = _rotary_tables(PAST_LEN + jnp.arange(dec_seq, dtype=jnp.int32))
    tabs_p = _retention_tables(PROMPT_BLOCK)
    tabs_s = _retention_tables(CHUNK)
    abias = _attn_bias(PROMPT_BLOCK)

    xs = x_sample.reshape(dec_batch * dec_seq, D_MODEL)
    ck = cache_swa_k.reshape(dec_batch, DEPTH, WINDOW, SWA_KV)
    cv = cache_swa_v.reshape(dec_batch, DEPTH, WINDOW, SWA_KV)

    xp = x_prompt
    kp_l, vp_l, ks_l, vs_l, rp_l, rs_l, gv_l = [], [], [], [], [], [], []
    for layer in range(DEPTH):
        sinks = swa_sinks[layer]
        xp, kp, vp, rp = _prompt_layer(layer, xp, sinks, cos_p, sin_p, weights, gb_tab, abias, tabs_p)
        xs, ks, vs, rs, gv = _sample_layer(layer, xs, sinks, cos_s, sin_s, ck, cv, state_ret, weights,
                                           gb_tab, tabs_s)
        kp_l.append(kp); vp_l.append(vp); ks_l.append(ks); vs_l.append(vs)
        rp_l.append(rp); rs_l.append(rs); gv_l.append(gv)

    kv5 = lambda a: jnp.stack(a, axis=1).reshape(a[0].shape[0], DEPTH, a[0].shape[1], SWA_KV_HEADS, HEAD_DIM)
    return (xp, xs.reshape(dec_batch, dec_seq, D_MODEL), kv5(kp_l), kv5(vp_l), kv5(ks_l), kv5(vs_l),
            jnp.stack(rp_l, axis=1), jnp.stack(rs_l, axis=1), jnp.stack(gv_l, axis=1))
```

```python
import functools
import math

import numpy as np
import jax
import jax.numpy as jnp
from jax import lax
from jax.experimental import pallas as pl
from jax.experimental.pallas import tpu as pltpu

D_MODEL = 1024
DEPTH = 4
PAST_LEN = 2048
CHUNK = 64
HEAD_DIM = 64
SWA_HEADS = 8
SWA_KV_HEADS = 2
WINDOW = 128
SWA_Q = SWA_HEADS * HEAD_DIM
SWA_KV = SWA_KV_HEADS * HEAD_DIM
GMLP_GROUPS = 4
GMLP_BLOCK = 128
GMLP_WIDTH = GMLP_GROUPS * HEAD_DIM
RET_HEADS = 4
RET_WIDTH = RET_HEADS * HEAD_DIM
ROPE_BASE = 10000.0
IN_WIDTH = SWA_Q + 2 * SWA_KV + 2 * GMLP_WIDTH + 4 * RET_WIDTH
D_FF = 2816
EPS = 1e-6
NEG = -1e30

OFF_QA, OFF_KA, OFF_VA, OFF_UB, OFF_VB, OFF_QC, OFF_KC, OFF_VC, OFF_GC = (
    0, 512, 640, 768, 1024, 1280, 1536, 1792, 2048)

LANES = 128
PROMPT_BLOCK = 128
PROMPT_TILE = 256
SAMPLE_SEQS = 4
FFN_CHUNKS = ((0, 768), (768, 1536), (1536, 2304), (2304, D_FF))
VMEM_LIMIT_BYTES = 56 * 1024 * 1024

HEADS_PLAIN = (0, 2, 5, 7)
HEADS_SWAPPED = (1, 3, 4, 6)

F32 = jnp.float32
BF16 = jnp.bfloat16
LOG2_E = 1.0 / math.log(2.0)


def _dot(a, b):
    return jnp.dot(a, b, preferred_element_type=F32)


def _dot_nt(a, b):
    return lax.dot_general(a, b, (((1,), (1,)), ((), ())), preferred_element_type=F32)


def _rms(x, g):
    return x * lax.rsqrt(jnp.mean(x * x, axis=-1, keepdims=True) + EPS) * g


def _gelu(x):
    c = math.sqrt(2.0 / math.pi)
    return x * (0.5 * (1.0 + jnp.tanh(c * (x + 0.044715 * (x * x * x)))))


def _silu(x):
    return x / (1.0 + jnp.exp2(x * (-LOG2_E)))


def _swa_half(q, k, v_ones, sink_ref, bias, swapped):
    rows = q.shape[0]
    lo = lax.broadcasted_iota(jnp.int32, (rows, LANES), 1) < HEAD_DIM
    scale = HEAD_DIM ** -0.5 * LOG2_E

    def half(pair, keep_lo):
        qp = q[:, pair * LANES:(pair + 1) * LANES] * scale
        return jnp.where(lo if keep_lo else jnp.logical_not(lo), qp, 0.0).astype(BF16)

    lhs = jnp.concatenate([half(0, not swapped), half(1, not swapped), half(2, swapped), half(3, swapped)], axis=0)
    heads = HEADS_SWAPPED if swapped else HEADS_PLAIN
    s = _dot_nt(lhs, k)
    if bias is not None:
        s = s + bias
    ps, sink_ps = [], []
    for g, h in enumerate(heads):
        sh = s[g * rows:(g + 1) * rows]
        sink = sink_ref[h] * LOG2_E
        m = jnp.maximum(jnp.max(sh, axis=-1, keepdims=True), sink)
        ps.append(jnp.exp2(sh - m).astype(BF16))
        sink_ps.append(jnp.exp2(sink - m))
    o = _dot(jnp.concatenate(ps, axis=0), v_ones)
    return jnp.concatenate([o[g * rows:(g + 1) * rows, :LANES] / (o[g * rows:(g + 1) * rows, LANES:] + sink_ps[g])
                            for g in range(len(heads))], axis=0)


def _with_ones(v):
    return jnp.concatenate([v, jnp.ones(v.shape, v.dtype)], axis=1)


def _swa_merge(o_p, o_s):
    r = o_p.shape[0] // 4
    lo = lax.broadcasted_iota(jnp.int32, (r, LANES), 1) < HEAD_DIM
    pairs = [
        jnp.where(lo, o_p[0:r], o_s[0:r]),
        jnp.where(lo, o_p[r:2 * r], o_s[r:2 * r]),
        jnp.where(lo, o_s[2 * r:3 * r], o_p[2 * r:3 * r]),
        jnp.where(lo, o_s[3 * r:4 * r], o_p[3 * r:4 * r]),
    ]
    return jnp.concatenate(pairs, axis=1)


def _gmlp_block(u_raw, v_raw, w_stack, bias_tab, ln_g, ln_b):
    rows = u_raw.shape[0]
    u = _gelu(u_raw)
    v = _gelu(v_raw)
    mu = jnp.mean(v, axis=-1, keepdims=True)
    d = v - mu
    var = jnp.mean(d * d, axis=-1, keepdims=True)
    vn = d * lax.rsqrt(var + EPS) * ln_g + ln_b
    lo = lax.broadcasted_iota(jnp.int32, (rows, LANES), 1) < HEAD_DIM
    vb = vn.astype(BF16)
    mixed = []
    for j in range(GMLP_GROUPS // 2):
        res = _dot(w_stack[j], vb[:, j * LANES:(j + 1) * LANES])
        mixed.append(jnp.where(lo, res[0:rows], res[rows:2 * rows]))
    mixed = jnp.concatenate(mixed, axis=1) + bias_tab
    return u * mixed, vn


def _rotary(x, cos_t, sin_t):
    rows = x.shape[0]
    first = (lax.broadcasted_iota(jnp.int32, (rows, LANES), 1) % HEAD_DIM) < (HEAD_DIM // 2)
    out = []
    for j in range(RET_WIDTH // LANES):
        sl = slice(j * LANES, (j + 1) * LANES)
        xh = x[:, sl]
        partner = jnp.where(first, pltpu.roll(xh, LANES - HEAD_DIM // 2, 1), pltpu.roll(xh, HEAD_DIM // 2, 1))
        out.append(xh * cos_t[:, sl] + partner * sin_t[:, sl])
    return jnp.concatenate(out, axis=1)


def _retention_block(qc, kc, vc, gc, cos_t, sin_t, state_bf, d4, xi_tab, zeta_tab, bd_mask, ret_g):
    rows = qc.shape[0]
    qr = _rotary(qc, cos_t, sin_t)
    kr = _rotary(kc, cos_t, sin_t) * (HEAD_DIM ** -0.5)
    head = lax.broadcasted_iota(jnp.int32, (rows, RET_WIDTH), 1) // HEAD_DIM
    q4 = jnp.concatenate([jnp.where(head == h, qr, 0.0) for h in range(RET_HEADS)], axis=0).astype(BF16)
    vb = vc.astype(BF16)
    s4 = _dot_nt(q4, kr.astype(BF16)) * d4
    o4 = _dot(s4.astype(BF16), vb)
    inner = o4[0:rows]
    for h in range(1, RET_HEADS):
        inner = jnp.where(head == h, o4[h * rows:(h + 1) * rows], inner)
    cross = _dot(qr.astype(BF16), state_bf) * xi_tab
    r = inner + cross
    kz_t = jnp.transpose(kr * zeta_tab).astype(BF16)
    kv = _dot(kz_t, vb)
    avg = (bd_mask * (1.0 / HEAD_DIM)).astype(BF16)
    mu = _dot(r.astype(BF16), avg)
    d = r - mu
    var = _dot((d * d).astype(BF16), avg)
    yn = d * lax.rsqrt(var + EPS) * ret_g
    return yn * _silu(gc), kv


def _mixer_pieces(y_ref, mix_ref, r0, rows, attn_half_fn, gmlp_fn, ret_fn, g_na, g_nb):
    sl = pl.ds(r0, rows)
    held = {}

    def attention_plain():
        held["plain"] = attn_half_fn(y_ref[sl, OFF_QA:OFF_QA + SWA_Q], False)

    def attention_swapped():
        a = _swa_merge(held.pop("plain"), attn_half_fn(y_ref[sl, OFF_QA:OFF_QA + SWA_Q], True))
        mix_ref[sl, 0:SWA_Q] = _rms(a, g_na).astype(BF16)

    def gmlp():
        b, vn = gmlp_fn(y_ref[sl, OFF_UB:OFF_UB + GMLP_WIDTH], y_ref[sl, OFF_VB:OFF_VB + GMLP_WIDTH])
        mix_ref[sl, SWA_Q:SWA_Q + GMLP_WIDTH] = _rms(b, g_nb).astype(BF16)
        return vn

    def retention():
        c, kv = ret_fn(y_ref[sl, OFF_QC:OFF_QC + RET_WIDTH], y_ref[sl, OFF_KC:OFF_KC + RET_WIDTH],
                       y_ref[sl, OFF_VC:OFF_VC + RET_WIDTH], y_ref[sl, OFF_GC:OFF_GC + RET_WIDTH])
        mix_ref[sl, SWA_Q + GMLP_WIDTH:D_MODEL] = c.astype(BF16)
        return kv

    return attention_plain, attention_swapped, gmlp, retention


def _gate_up(h2, c, wg_ref, wu_ref):
    c0, c1 = FFN_CHUNKS[c]
    return (_silu(_dot(h2, wg_ref[:, c0:c1])) * _dot(h2, wu_ref[:, c0:c1])).astype(BF16)


def _down(f, act, c, wd_ref):
    c0, c1 = FFN_CHUNKS[c]
    part = _dot(act, wd_ref[c0:c1, :])
    return part if f is None else f + part


def _ffn_tail(x, mix, w_out_ref, wg_ref, wu_ref, wd_ref, g_post_mix, g_pre_ffn, g_post_ffn):
    x1 = x + _rms(_dot(mix, w_out_ref[...]), g_post_mix)
    h2 = _rms(x1, g_pre_ffn).astype(BF16)
    f = None
    for c in range(len(FFN_CHUNKS)):
        f = _down(f, _gate_up(h2, c, wg_ref, wu_ref), c, wd_ref)
    return x1 + _rms(f, g_post_ffn)


def _store_diag_blocks(out_ref, m):
    for h in range(RET_HEADS):
        out_ref[h] = m[h * HEAD_DIM:(h + 1) * HEAD_DIM, h * HEAD_DIM:(h + 1) * HEAD_DIM]


def _prompt_kernel(tiles_per_seq, n_tiles,
                   sink_ref, x_ref, xres_ref, cos_ref, sin_ref, w_in_ref, w_out_ref, wg_ref, wu_ref, wd_ref,
                   g_pre_mix, g_post_mix, g_pre_ffn, g_post_ffn, g_na, g_nb, g_ret, ln_g, ln_b,
                   gw_ref, gb_ref, abias_ref, d4_ref, xi_ref, zeta_ref, dec_ref, bd_ref,
                   xo_ref, ko_ref, vo_ref, so_ref,
                   y_scr, mix_scr, kw_scr, kws_scr, vw_scr, vws_scr, st_scr):
    tile = x_ref.shape[0]
    blk = PROMPT_BLOCK
    s = pl.program_id(0)
    t = jnp.clip(s - 1, 0, n_tiles - 1) % tiles_per_seq
    live = jnp.logical_and(s >= 1, s <= n_tiles)

    @pl.when(s == 0)
    def _():
        mix_scr[...] = jnp.zeros(mix_scr.shape, BF16)
        y_scr[...] = jnp.zeros(y_scr.shape, F32)

    @pl.when(t == 0)
    def _():
        for ref in (kw_scr, kws_scr, vw_scr, vws_scr):
            ref[0:blk, :] = jnp.zeros((blk, ref.shape[1]), BF16)
        st_scr[...] = jnp.zeros(st_scr.shape, F32)

    @pl.when(jnp.logical_and(t > 0, live))
    def _():
        for ref in (kw_scr, kws_scr, vw_scr, vws_scr):
            ref[0:blk, :] = ref[tile:tile + blk, :]

    y_cur = y_scr.at[(s + 1) % 2]
    y_next = y_scr.at[s % 2]

    merged = _dot(mix_scr[...], w_out_ref[...])

    h = _rms(x_ref[...], g_pre_mix[...]).astype(BF16)

    k_new = y_cur[:, OFF_KA:OFF_KA + SWA_KV]
    v_new = y_cur[:, OFF_VA:OFF_VA + SWA_KV]
    kw_scr[blk:blk + tile, :] = k_new.astype(BF16)
    kws_scr[blk:blk + tile, :] = pltpu.roll(k_new, HEAD_DIM, 1).astype(BF16)
    vw_scr[blk:blk + tile, :] = _with_ones(v_new.astype(BF16))
    vws_scr[blk:blk + tile, :] = _with_ones(pltpu.roll(v_new, HEAD_DIM, 1).astype(BF16))
    ko_ref[...] = k_new[tile - WINDOW:tile]
    vo_ref[...] = v_new[tile - WINDOW:tile]

    ri = lax.broadcasted_iota(jnp.int32, (blk, blk), 0)
    ci = lax.broadcasted_iota(jnp.int32, (blk, blk), 1)
    keep = jnp.logical_not(jnp.logical_and(ri < CHUNK, ci >= CHUNK))
    w_stack = [jnp.concatenate([jnp.where(keep, gw_ref[2 * j], 0.0), jnp.where(keep, gw_ref[2 * j + 1], 0.0)],
                               axis=0).astype(BF16) for j in range(GMLP_GROUPS // 2)]

    col = lax.broadcasted_iota(jnp.int32, abias_ref.shape, 1)
    first_bias = abias_ref[...] + jnp.where(jnp.logical_and(col < blk, t == 0), NEG, 0.0)

    def mixer_pieces(j):
        r0 = j * blk
        win = pl.ds(r0, 2 * blk)
        bias = first_bias if j == 0 else abias_ref[...]

        def attn_half_fn(q, swapped):
            k_ref, v_ref = (kws_scr, vws_scr) if swapped else (kw_scr, vw_scr)
            return _swa_half(q, k_ref[win, :], v_ref[win, :], sink_ref, bias, swapped)

        gmlp_fn = lambda u, v: _gmlp_block(u, v, w_stack, gb_ref[...], ln_g[...], ln_b[...])
        ret_fn = lambda q, k, v, g: _retention_block(
            q, k, v, g, cos_ref[pl.ds(r0, blk), :], sin_ref[pl.ds(r0, blk), :], st_scr[...].astype(BF16),
            d4_ref[...], xi_ref[...], zeta_ref[...], bd_ref[...], g_ret[...])
        return _mixer_pieces(y_cur, mix_scr, r0, blk, attn_half_fn, gmlp_fn, ret_fn, g_na[...], g_nb[...])

    def advance_state(kv):
        state = st_scr[...]
        st_scr[...] = jnp.where(live, state * dec_ref[...] + kv * bd_ref[...], state)

    assert tile // blk == 2 and len(FFN_CHUNKS) == 4
    attn_plain0, attn_swapped0, gmlp0, retention0 = mixer_pieces(0)
    attn_plain1, attn_swapped1, gmlp1, retention1 = mixer_pieces(1)
    gate_up = lambda c: _gate_up(h2, c, wg_ref, wu_ref)
    down = lambda f, act, c: _down(f, act, c, wd_ref)

    y_next[...] = _dot(h, w_in_ref[...])
    x1 = xres_ref[...] + _rms(merged, g_post_mix[...])
    h2 = _rms(x1, g_pre_ffn[...]).astype(BF16)
    act = gate_up(0)
    attn_plain0()
    f = down(None, act, 0)
    act = gate_up(1)
    attn_swapped0()
    f = down(f, act, 1)
    gmlp0()
    act = gate_up(2)
    advance_state(retention0())
    attn_plain1()
    f = down(f, act, 2)
    attn_swapped1()
    act = gate_up(3)
    gmlp1()
    advance_state(retention1())
    f = down(f, act, 3)
    _store_diag_blocks(so_ref, st_scr[...])
    xo_ref[...] = x1 + _rms(f, g_post_ffn[...])


def _sample_kernel(sink_ref, x_ref, cos_ref, sin_ref, ck_ref, cv_ref, st0_ref,
                   w_in_ref, w_out_ref, wg_ref, wu_ref, wd_ref,
                   g_pre_mix, g_post_mix, g_pre_ffn, g_post_ffn, g_na, g_nb, g_ret, ln_g, ln_b,
                   gw_ref, gb_ref, d4_ref, xi_ref, zeta_ref, dec_ref, bd_ref,
                   xo_ref, ko_ref, vo_ref, so_ref, gv_ref,
                   y_scr, mix_scr, kw_scr, kws_scr, vw_scr, vws_scr):
    seq = CHUNK
    nseq = x_ref.shape[0] // seq

    x = x_ref[...]
    h = _rms(x, g_pre_mix[...]).astype(BF16)
    y_scr[...] = _dot(h, w_in_ref[...])

    w_stack = [jnp.concatenate([gw_ref[2 * j, 0:seq, 0:seq], gw_ref[2 * j + 1, 0:seq, 0:seq]],
                               axis=0).astype(BF16) for j in range(GMLP_GROUPS // 2)]

    for s in range(nseq):
        r0 = s * seq
        sl = pl.ds(r0, seq)
        k_new = y_scr[sl, OFF_KA:OFF_KA + SWA_KV]
        v_new = y_scr[sl, OFF_VA:OFF_VA + SWA_KV]
        ko_ref[s] = k_new
        vo_ref[s] = v_new
        k_all = jnp.concatenate([ck_ref[s], k_new], axis=0)
        v_all = jnp.concatenate([cv_ref[s], v_new], axis=0)
        kw_scr[s] = k_all.astype(BF16)
        kws_scr[s] = pltpu.roll(k_all, HEAD_DIM, 1).astype(BF16)
        vw_scr[s] = _with_ones(v_all.astype(BF16))
        vws_scr[s] = _with_ones(pltpu.roll(v_all, HEAD_DIM, 1).astype(BF16))

        st0 = st0_ref[s]
        zero = jnp.zeros((HEAD_DIM, HEAD_DIM), F32)
        state = jnp.concatenate(
            [jnp.concatenate([st0[h] if g == h else zero for g in range(RET_HEADS)], axis=1)
             for h in range(RET_HEADS)], axis=0)

        def attn_half_fn(q, swapped):
            k_ref, v_ref = (kws_scr, vws_scr) if swapped else (kw_scr, vw_scr)
            return _swa_half(q, k_ref[s], v_ref[s], sink_ref, None, swapped)

        gmlp_fn = lambda u, v: _gmlp_block(u, v, w_stack, gb_ref[...], ln_g[...], ln_b[...])
        ret_fn = lambda q, k, v, g: _retention_block(
            q, k, v, g, cos_ref[...], sin_ref[...], state.astype(BF16),
            d4_ref[...], xi_ref[...], zeta_ref[...], bd_ref[...], g_ret[...])
        attn_plain, attn_swapped, gmlp, retention = _mixer_pieces(y_scr, mix_scr, r0, seq, attn_half_fn,
                                                                  gmlp_fn, ret_fn, g_na[...], g_nb[...])
        attn_plain()
        attn_swapped()
        gv_ref[s] = gmlp()
        _store_diag_blocks(so_ref.at[s], state * dec_ref[...] + retention())

    xo_ref[...] = _ffn_tail(x, mix_scr[...], w_out_ref, wg_ref, wu_ref, wd_ref,
                            g_post_mix[...], g_pre_ffn[...], g_post_ffn[...])


def _rotary_tables(pos):
    half = HEAD_DIM // 2
    inv = ROPE_BASE ** (-np.arange(half, dtype=np.float64) / half)
    ang = np.asarray(pos, np.float64)[:, None] * inv[None, :]
    cos = np.cos(ang)
    sin = np.sin(ang)
    cos_t = np.tile(np.concatenate([cos, cos], axis=1), (1, RET_HEADS))
    sin_t = np.tile(np.concatenate([-sin, sin], axis=1), (1, RET_HEADS))
    return jnp.asarray(cos_t, F32), jnp.asarray(sin_t, F32)


def _retention_tables(block):
    logg = np.log(1.0 - 2.0 ** (-5.0 - np.arange(RET_HEADS, dtype=np.float64)))
    idx = np.arange(block, dtype=np.float64)
    rel = idx[:, None] - idx[None, :]
    decay = np.where(rel >= 0, np.exp(logg[:, None, None] * np.maximum(rel, 0.0)), 0.0)
    d4 = decay.reshape(RET_HEADS * block, block)
    xi = np.exp(logg[:, None] * (idx + 1.0))
    zeta = np.exp(logg[:, None] * (block - 1.0 - idx))
    xi_tab = np.repeat(xi.T, HEAD_DIM, axis=1)
    zeta_tab = np.repeat(zeta.T, HEAD_DIM, axis=1)
    chunk_decay = np.repeat(np.exp(logg * block), HEAD_DIM)
    hid = np.arange(RET_WIDTH) // HEAD_DIM
    bd = (hid[:, None] == hid[None, :]).astype(np.float64)
    dec_tab = bd * chunk_decay[:, None]
    return tuple(jnp.asarray(a, F32) for a in (d4, xi_tab, zeta_tab, dec_tab, bd))


def _attn_bias(block):
    r = np.arange(block)[:, None]
    c = np.arange(2 * block)[None, :]
    visible = np.where(r < CHUNK, c < 2 * block - CHUNK, c >= CHUNK)
    return jnp.asarray(np.tile(np.where(visible, 0.0, NEG), (4, 1)), F32)


def _layer_weight(shape, layer):
    return pl.BlockSpec((None,) + shape, lambda *_: (layer,) + (0,) * len(shape), pipeline_mode=pl.Buffered(1))


def _const(shape):
    return pl.BlockSpec(shape, lambda *_: (0,) * len(shape), pipeline_mode=pl.Buffered(1))


def _layer_vec(width, layer):
    return pl.BlockSpec((None, 1, width), lambda *_: (layer, 0, 0), pipeline_mode=pl.Buffered(1))


def _weight_specs(layer):
    return [
        _layer_weight((D_MODEL, IN_WIDTH), layer),
        _layer_weight((D_MODEL, D_MODEL), layer),
        _layer_weight((D_MODEL, D_FF), layer),
        _layer_weight((D_MODEL, D_FF), layer),
        _layer_weight((D_FF, D_MODEL), layer),
        _layer_vec(D_MODEL, layer), _layer_vec(D_MODEL, layer), _layer_vec(D_MODEL, layer),
        _layer_vec(D_MODEL, layer),
        _layer_vec(SWA_Q, layer), _layer_vec(GMLP_WIDTH, layer), _layer_vec(RET_WIDTH, layer),
        _layer_vec(GMLP_WIDTH, layer), _layer_vec(GMLP_WIDTH, layer),
        _layer_weight((GMLP_GROUPS, GMLP_BLOCK, GMLP_BLOCK), layer),
    ]


def _prompt_layer(layer, x, sinks, cos_t, sin_t, weights, gb_tab, abias, ret_tabs):
    batch, seq, _ = x.shape
    tile, blk = PROMPT_TILE, PROMPT_BLOCK
    d4, xi_tab, zeta_tab, dec_tab, bd = ret_tabs
    tiles_per_seq = seq // tile
    n_tiles = batch * tiles_per_seq
    stage = lambda lag: (lambda s: jnp.clip(s - lag, 0, n_tiles - 1))
    proj_tile, mix_tile, ffn_tile = stage(0), stage(1), stage(2)
    row_block = lambda tile_of: (lambda s: (tile_of(s) // tiles_per_seq, tile_of(s) % tiles_per_seq, 0))
    in_specs = [
        pl.BlockSpec(memory_space=pltpu.SMEM),
        pl.BlockSpec((None, tile, D_MODEL), row_block(proj_tile)),
        pl.BlockSpec((None, tile, D_MODEL), row_block(ffn_tile)),
        pl.BlockSpec((tile, RET_WIDTH), lambda s: (mix_tile(s) % tiles_per_seq, 0)),
        pl.BlockSpec((tile, RET_WIDTH), lambda s: (mix_tile(s) % tiles_per_seq, 0)),
        *_weight_specs(layer),
        pl.BlockSpec((None, blk, GMLP_WIDTH), lambda s: (layer, 0, 0), pipeline_mode=pl.Buffered(1)),
        _const(abias.shape), _const(d4.shape), _const(xi_tab.shape), _const(zeta_tab.shape),
        _const(dec_tab.shape), _const(bd.shape),
    ]
    out_shape = [
        jax.ShapeDtypeStruct((batch, seq, D_MODEL), F32),
        jax.ShapeDtypeStruct((batch, WINDOW, SWA_KV), F32),
        jax.ShapeDtypeStruct((batch, WINDOW, SWA_KV), F32),
        jax.ShapeDtypeStruct((batch, RET_HEADS, HEAD_DIM, HEAD_DIM), F32),
    ]
    out_specs = [
        pl.BlockSpec((None, tile, D_MODEL), row_block(ffn_tile)),
        pl.BlockSpec((None, WINDOW, SWA_KV), lambda s: (mix_tile(s) // tiles_per_seq, 0, 0)),
        pl.BlockSpec((None, WINDOW, SWA_KV), lambda s: (mix_tile(s) // tiles_per_seq, 0, 0)),
        pl.BlockSpec((None, RET_HEADS, HEAD_DIM, HEAD_DIM), lambda s: (mix_tile(s) // tiles_per_seq, 0, 0, 0)),
    ]
    scratch = [
        pltpu.VMEM((2, tile, IN_WIDTH), F32),
        pltpu.VMEM((tile, D_MODEL), BF16),
        pltpu.VMEM((tile + blk, SWA_KV), BF16), pltpu.VMEM((tile + blk, SWA_KV), BF16),
        pltpu.VMEM((tile + blk, 2 * SWA_KV), BF16), pltpu.VMEM((tile + blk, 2 * SWA_KV), BF16),
        pltpu.VMEM((RET_WIDTH, RET_WIDTH), F32),
    ]
    return pl.pallas_call(
        functools.partial(_prompt_kernel, tiles_per_seq, n_tiles),
        grid=(n_tiles + 2,),
        in_specs=in_specs, out_specs=out_specs, out_shape=out_shape, scratch_shapes=scratch,
        compiler_params=pltpu.CompilerParams(dimension_semantics=("arbitrary",),
                                             vmem_limit_bytes=VMEM_LIMIT_BYTES),
        name=f"prompt_layer{layer}",
    )(sinks, x, x, cos_t, sin_t, *weights, gb_tab, abias, d4, xi_tab, zeta_tab, dec_tab, bd)


def _sample_layer(layer, x, sinks, cos_t, sin_t, cache_k, cache_v, state0, weights, gb_tab, ret_tabs):
    rows = x.shape[0]
    nseq = SAMPLE_SEQS
    tile = nseq * CHUNK
    n_all = rows // CHUNK
    d4, xi_tab, zeta_tab, dec_tab, bd = ret_tabs
    in_specs = [
        pl.BlockSpec(memory_space=pltpu.SMEM),
        pl.BlockSpec((tile, D_MODEL), lambda i: (i, 0)),
        _const(cos_t.shape), _const(sin_t.shape),
        pl.BlockSpec((nseq, None, WINDOW, SWA_KV), lambda i: (i, layer, 0, 0)),
        pl.BlockSpec((nseq, None, WINDOW, SWA_KV), lambda i: (i, layer, 0, 0)),
        pl.BlockSpec((nseq, None, RET_HEADS, HEAD_DIM, HEAD_DIM), lambda i: (i, layer, 0, 0, 0)),
        *_weight_specs(layer),
        pl.BlockSpec((None, CHUNK, GMLP_WIDTH), lambda i: (layer, 0, 0), pipeline_mode=pl.Buffered(1)),
        _const(d4.shape), _const(xi_tab.shape), _const(zeta_tab.shape), _const(dec_tab.shape), _const(bd.shape),
    ]
    out_shape = [
        jax.ShapeDtypeStruct((rows, D_MODEL), F32),
        jax.ShapeDtypeStruct((n_all, CHUNK, SWA_KV), F32),
        jax.ShapeDtypeStruct((n_all, CHUNK, SWA_KV), F32),
        jax.ShapeDtypeStruct((n_all, RET_HEADS, HEAD_DIM, HEAD_DIM), F32),
        jax.ShapeDtypeStruct((n_all, CHUNK, GMLP_WIDTH), F32),
    ]
    out_specs = [
        pl.BlockSpec((tile, D_MODEL), lambda i: (i, 0)),
        pl.BlockSpec((nseq, CHUNK, SWA_KV), lambda i: (i, 0, 0)),
        pl.BlockSpec((nseq, CHUNK, SWA_KV), lambda i: (i, 0, 0)),
        pl.BlockSpec((nseq, RET_HEADS, HEAD_DIM, HEAD_DIM), lambda i: (i, 0, 0, 0)),
        pl.BlockSpec((nseq, CHUNK, GMLP_WIDTH), lambda i: (i, 0, 0)),
    ]
    win = WINDOW + CHUNK
    scratch = [
        pltpu.VMEM((tile, IN_WIDTH), F32),
        pltpu.VMEM((tile, D_MODEL), BF16),
        pltpu.VMEM((nseq, win, SWA_KV), BF16), pltpu.VMEM((nseq, win, SWA_KV), BF16),
        pltpu.VMEM((nseq, win, 2 * SWA_KV), BF16), pltpu.VMEM((nseq, win, 2 * SWA_KV), BF16),
    ]
    return pl.pallas_call(
        _sample_kernel,
        grid=(n_all // nseq,),
        in_specs=in_specs, out_specs=out_specs, out_shape=out_shape, scratch_shapes=scratch,
        compiler_params=pltpu.CompilerParams(dimension_semantics=("arbitrary",),
                                             vmem_limit_bytes=VMEM_LIMIT_BYTES),
        name=f"sample_layer{layer}",
    )(sinks, x, cos_t, sin_t, cache_k, cache_v, state0, *weights, gb_tab, d4, xi_tab, zeta_tab, dec_tab, bd)


def kernel(x_prompt, x_sample, cache_swa_k, cache_swa_v, state_ret, w_in, w_out, swa_sinks,
           gmlp_w, gmlp_b, gmlp_ln_g, gmlp_ln_b, norm_a_g, norm_b_g, ret_norm_g,
           ln_pre_mix, ln_post_mix, ln_pre_ffn, ln_post_ffn, w_gate, w_up, w_down):
    batch, seq, _ = x_prompt.shape
    dec_batch, dec_seq, _ = x_sample.shape
    assert dec_seq == CHUNK and seq % PROMPT_TILE == 0 and dec_batch % SAMPLE_SEQS == 0
    assert cache_swa_k.shape[2] == WINDOW

    vec = lambda a: a.reshape(DEPTH, 1, a.shape[-1])
    weights = [
        w_in.astype(BF16), w_out.astype(BF16), w_gate.astype(BF16), w_up.astype(BF16), w_down.astype(BF16),
        vec(ln_pre_mix), vec(ln_post_mix), vec(ln_pre_ffn), vec(ln_post_ffn),
        vec(norm_a_g), vec(norm_b_g), vec(ret_norm_g), vec(gmlp_ln_g), vec(gmlp_ln_b),
        gmlp_w,
    ]
    gb_tab = jnp.repeat(jnp.swapaxes(gmlp_b, 1, 2), HEAD_DIM, axis=2)

    cos_p, sin_p = _rotary_tables(np.arange(seq))
    cos_s, sin_s = _rotary_tables(PAST_LEN + np.arange(dec_seq))
    tabs_p = _retention_tables(PROMPT_BLOCK)
    tabs_s = _retention_tables(CHUNK)
    abias = _attn_bias(PROMPT_BLOCK)

    xs = x_sample.reshape(dec_batch * dec_seq, D_MODEL)
    ck = cache_swa_k.reshape(dec_batch, DEPTH, WINDOW, SWA_KV)
    cv = cache_swa_v.reshape(dec_batch, DEPTH, WINDOW, SWA_KV)

    xp = x_prompt
    kp_l, vp_l, ks_l, vs_l, rp_l, rs_l, gv_l = [], [], [], [], [], [], []
    for layer in range(DEPTH):
        sinks = swa_sinks[layer]
        xp, kp, vp, rp = _prompt_layer(layer, xp, sinks, cos_p, sin_p, weights, gb_tab, abias, tabs_p)
        xs, ks, vs, rs, gv = _sample_layer(layer, xs, sinks, cos_s, sin_s, ck, cv, state_ret, weights,
                                           gb_tab, tabs_s)
        kp_l.append(kp); vp_l.append(vp); ks_l.append(ks); vs_l.append(vs)
        rp_l.append(rp); rs_l.append(rs); gv_l.append(gv)

    kv5 = lambda a: jnp.stack(a, axis=1).reshape(a[0].shape[0], DEPTH, a[0].shape[1], SWA_KV_HEADS, HEAD_DIM)
    return (xp, xs.reshape(dec_batch, dec_seq, D_MODEL), kv5(kp_l), kv5(vp_l), kv5(ks_l), kv5(vs_l),
            jnp.stack(rp_l, axis=1), jnp.stack(rs_l, axis=1), jnp.stack(gv_l, axis=1))
```

```python
import functools
import math

import numpy as np
import jax
import jax.numpy as jnp
from jax import lax
from jax.experimental import pallas as pl
from jax.experimental.pallas import tpu as pltpu

D_MODEL = 1024
DEPTH = 4
PAST_LEN = 2048
CHUNK = 64
HEAD_DIM = 64
SWA_HEADS = 8
SWA_KV_HEADS = 2
WINDOW = 128
SWA_Q = SWA_HEADS * HEAD_DIM
SWA_KV = SWA_KV_HEADS * HEAD_DIM
GMLP_GROUPS = 4
GMLP_BLOCK = 128
GMLP_WIDTH = GMLP_GROUPS * HEAD_DIM
RET_HEADS = 4
RET_WIDTH = RET_HEADS * HEAD_DIM
ROPE_BASE = 10000.0
IN_WIDTH = SWA_Q + 2 * SWA_KV + 2 * GMLP_WIDTH + 4 * RET_WIDTH
D_FF = 2816
EPS = 1e-6
NEG = -1e30

OFF_QA, OFF_KA, OFF_VA, OFF_UB, OFF_VB, OFF_QC, OFF_KC, OFF_VC, OFF_GC = (
    0, 512, 640, 768, 1024, 1280, 1536, 1792, 2048)

LANES = 128
PROMPT_BLOCK = 128
PROMPT_TILE = 256
SAMPLE_SEQS = 4
FFN_CHUNKS = ((0, 768), (768, 1536), (1536, 2304), (2304, D_FF))
VMEM_LIMIT_BYTES = 56 * 1024 * 1024

HEADS_PLAIN = (0, 2, 5, 7)
HEADS_SWAPPED = (1, 3, 4, 6)

F32 = jnp.float32
BF16 = jnp.bfloat16
LOG2_E = 1.0 / math.log(2.0)


def _dot(a, b):
    return jnp.dot(a, b, preferred_element_type=F32)


def _dot_nt(a, b):
    return lax.dot_general(a, b, (((1,), (1,)), ((), ())), preferred_element_type=F32)


def _rms(x, g):
    return x * lax.rsqrt(jnp.mean(x * x, axis=-1, keepdims=True) + EPS) * g


def _gelu(x):
    c = math.sqrt(2.0 / math.pi)
    return x * (0.5 * (1.0 + jnp.tanh(c * (x + 0.044715 * (x * x * x)))))


def _silu(x):
    return x / (1.0 + jnp.exp2(x * (-LOG2_E)))


def _swa_half(q, k, v_ones, sink_ref, bias, swapped):
    rows = q.shape[0]
    lo = lax.broadcasted_iota(jnp.int32, (rows, LANES), 1) < HEAD_DIM
    scale = HEAD_DIM ** -0.5 * LOG2_E

    def half(pair, keep_lo):
        qp = q[:, pair * LANES:(pair + 1) * LANES] * scale
        return jnp.where(lo if keep_lo else jnp.logical_not(lo), qp, 0.0).astype(BF16)

    lhs = jnp.concatenate([half(0, not swapped), half(1, not swapped), half(2, swapped), half(3, swapped)], axis=0)
    heads = HEADS_SWAPPED if swapped else HEADS_PLAIN
    s = _dot_nt(lhs, k)
    if bias is not None:
        s = s + bias
    ps, sink_ps = [], []
    for g, h in enumerate(heads):
        sh = s[g * rows:(g + 1) * rows]
        sink = sink_ref[h] * LOG2_E
        m = jnp.maximum(jnp.max(sh, axis=-1, keepdims=True), sink)
        ps.append(jnp.exp2(sh - m).astype(BF16))
        sink_ps.append(jnp.exp2(sink - m))
    o = _dot(jnp.concatenate(ps, axis=0), v_ones)
    return jnp.concatenate([o[g * rows:(g + 1) * rows, :LANES] / (o[g * rows:(g + 1) * rows, LANES:] + sink_ps[g])
                            for g in range(len(heads))], axis=0)


def _with_ones(v):
    return jnp.concatenate([v, jnp.ones(v.shape, v.dtype)], axis=1)


def _swa_merge(o_p, o_s):
    r = o_p.shape[0] // 4
    lo = lax.broadcasted_iota(jnp.int32, (r, LANES), 1) < HEAD_DIM
    pairs = [
        jnp.where(lo, o_p[0:r], o_s[0:r]),
        jnp.where(lo, o_p[r:2 * r], o_s[r:2 * r]),
        jnp.where(lo, o_s[2 * r:3 * r], o_p[2 * r:3 * r]),
        jnp.where(lo, o_s[3 * r:4 * r], o_p[3 * r:4 * r]),
    ]
    return jnp.concatenate(pairs, axis=1)


def _gmlp_block(u_raw, v_raw, w_stack, bias_tab, ln_g, ln_b):
    rows = u_raw.shape[0]
    u = _gelu(u_raw)
    v = _gelu(v_raw)
    mu = jnp.mean(v, axis=-1, keepdims=True)
    d = v - mu
    var = jnp.mean(d * d, axis=-1, keepdims=True)
    vn = d * lax.rsqrt(var + EPS) * ln_g + ln_b
    lo = lax.broadcasted_iota(jnp.int32, (rows, LANES), 1) < HEAD_DIM
    vb = vn.astype(BF16)
    mixed = []
    for j in range(GMLP_GROUPS // 2):
        res = _dot(w_stack[j], vb[:, j * LANES:(j + 1) * LANES])
        mixed.append(jnp.where(lo, res[0:rows], res[rows:2 * rows]))
    mixed = jnp.concatenate(mixed, axis=1) + bias_tab
    return u * mixed, vn


def _rotary(x, cos_t, sin_t):
    rows = x.shape[0]
    first = (lax.broadcasted_iota(jnp.int32, (rows, LANES), 1) % HEAD_DIM) < (HEAD_DIM // 2)
    out = []
    for j in range(RET_WIDTH // LANES):
        sl = slice(j * LANES, (j + 1) * LANES)
        xh = x[:, sl]
        partner = jnp.where(first, pltpu.roll(xh, LANES - HEAD_DIM // 2, 1), pltpu.roll(xh, HEAD_DIM // 2, 1))
        out.append(xh * cos_t[:, sl] + partner * sin_t[:, sl])
    return jnp.concatenate(out, axis=1)


def _retention_block(qc, kc, vc, gc, cos_t, sin_t, state_bf, d4, xi_tab, zeta_tab, bd_mask, ret_g):
    rows = qc.shape[0]
    qr = _rotary(qc, cos_t, sin_t)
    kr = _rotary(kc, cos_t, sin_t) * (HEAD_DIM ** -0.5)
    head = lax.broadcasted_iota(jnp.int32, (rows, RET_WIDTH), 1) // HEAD_DIM
    q4 = jnp.concatenate([jnp.where(head == h, qr, 0.0) for h in range(RET_HEADS)], axis=0).astype(BF16)
    vb = vc.astype(BF16)
    s4 = _dot_nt(q4, kr.astype(BF16)) * d4
    o4 = _dot(s4.astype(BF16), vb)
    inner = o4[0:rows]
    for h in range(1, RET_HEADS):
        inner = jnp.where(head == h, o4[h * rows:(h + 1) * rows], inner)
    cross = _dot(qr.astype(BF16), state_bf) * xi_tab
    r = inner + cross
    kz_t = jnp.transpose(kr * zeta_tab).astype(BF16)
    kv = _dot(kz_t, vb)
    avg = (bd_mask * (1.0 / HEAD_DIM)).astype(BF16)
    mu = _dot(r.astype(BF16), avg)
    d = r - mu
    var = _dot((d * d).astype(BF16), avg)
    yn = d * lax.rsqrt(var + EPS) * ret_g
    return yn * _silu(gc), kv


def _mixer_pieces(y_ref, mix_ref, r0, rows, attn_half_fn, gmlp_fn, ret_fn, g_na, g_nb):
    sl = pl.ds(r0, rows)
    held = {}

    def attention_plain():
        held["plain"] = attn_half_fn(y_ref[sl, OFF_QA:OFF_QA + SWA_Q], False)

    def attention_swapped():
        a = _swa_merge(held.pop("plain"), attn_half_fn(y_ref[sl, OFF_QA:OFF_QA + SWA_Q], True))
        mix_ref[sl, 0:SWA_Q] = _rms(a, g_na).astype(BF16)

    def gmlp():
        b, vn = gmlp_fn(y_ref[sl, OFF_UB:OFF_UB + GMLP_WIDTH], y_ref[sl, OFF_VB:OFF_VB + GMLP_WIDTH])
        mix_ref[sl, SWA_Q:SWA_Q + GMLP_WIDTH] = _rms(b, g_nb).astype(BF16)
        return vn

    def retention():
        c, kv = ret_fn(y_ref[sl, OFF_QC:OFF_QC + RET_WIDTH], y_ref[sl, OFF_KC:OFF_KC + RET_WIDTH],
                       y_ref[sl, OFF_VC:OFF_VC + RET_WIDTH], y_ref[sl, OFF_GC:OFF_GC + RET_WIDTH])
        mix_ref[sl, SWA_Q + GMLP_WIDTH:D_MODEL] = c.astype(BF16)
        return kv

    return attention_plain, attention_swapped, gmlp, retention


def _gate_up(h2, c, wg_ref, wu_ref):
    c0, c1 = FFN_CHUNKS[c]
    return (_silu(_dot(h2, wg_ref[:, c0:c1])) * _dot(h2, wu_ref[:, c0:c1])).astype(BF16)


def _down(f, act, c, wd_ref):
    c0, c1 = FFN_CHUNKS[c]
    part = _dot(act, wd_ref[c0:c1, :])
    return part if f is None else f + part


def _ffn_tail(x, mix, w_out_ref, wg_ref, wu_ref, wd_ref, g_post_mix, g_pre_ffn, g_post_ffn):
    x1 = x + _rms(_dot(mix, w_out_ref[...]), g_post_mix)
    h2 = _rms(x1, g_pre_ffn).astype(BF16)
    f = None
    for c in range(len(FFN_CHUNKS)):
        f = _down(f, _gate_up(h2, c, wg_ref, wu_ref), c, wd_ref)
    return x1 + _rms(f, g_post_ffn)


def _store_diag_blocks(out_ref, m):
    for h in range(RET_HEADS):
        out_ref[h] = m[h * HEAD_DIM:(h + 1) * HEAD_DIM, h * HEAD_DIM:(h + 1) * HEAD_DIM]


def _prompt_kernel(tiles_per_seq, n_tiles,
                   sink_ref, x_ref, xres_ref, cos_ref, sin_ref, w_in_ref, w_out_ref, wg_ref, wu_ref, wd_ref,
                   g_pre_mix, g_post_mix, g_pre_ffn, g_post_ffn, g_na, g_nb, g_ret, ln_g, ln_b,
                   gw_ref, gb_ref, abias_ref, d4_ref, xi_ref, zeta_ref, dec_ref, bd_ref,
                   xo_ref, ko_ref, vo_ref, so_ref,
                   y_scr, mix_scr, kw_scr, kws_scr, vw_scr, vws_scr, st_scr):
    tile = x_ref.shape[0]
    blk = PROMPT_BLOCK
    s = pl.program_id(0)
    t = jnp.clip(s - 1, 0, n_tiles - 1) % tiles_per_seq
    live = jnp.logical_and(s >= 1, s <= n_tiles)

    @pl.when(s == 0)
    def _():
        mix_scr[...] = jnp.zeros(mix_scr.shape, BF16)
        y_scr[...] = jnp.zeros(y_scr.shape, F32)

    @pl.when(t == 0)
    def _():
        for ref in (kw_scr, kws_scr, vw_scr, vws_scr):
            ref[0:blk, :] = jnp.zeros((blk, ref.shape[1]), BF16)
        st_scr[...] = jnp.zeros(st_scr.shape, F32)

    @pl.when(jnp.logical_and(t > 0, live))
    def _():
        for ref in (kw_scr, kws_scr, vw_scr, vws_scr):
            ref[0:blk, :] = ref[tile:tile + blk, :]

    y_cur = y_scr.at[(s + 1) % 2]
    y_next = y_scr.at[s % 2]

    merged = _dot(mix_scr[...], w_out_ref[...])

    h = _rms(x_ref[...], g_pre_mix[...]).astype(BF16)

    k_new = y_cur[:, OFF_KA:OFF_KA + SWA_KV]
    v_new = y_cur[:, OFF_VA:OFF_VA + SWA_KV]
    kw_scr[blk:blk + tile, :] = k_new.astype(BF16)
    kws_scr[blk:blk + tile, :] = pltpu.roll(k_new, HEAD_DIM, 1).astype(BF16)
    vw_scr[blk:blk + tile, :] = _with_ones(v_new.astype(BF16))
    vws_scr[blk:blk + tile, :] = _with_ones(pltpu.roll(v_new, HEAD_DIM, 1).astype(BF16))
    ko_ref[...] = k_new[tile - WINDOW:tile]
    vo_ref[...] = v_new[tile - WINDOW:tile]

    ri = lax.broadcasted_iota(jnp.int32, (blk, blk), 0)
    ci = lax.broadcasted_iota(jnp.int32, (blk, blk), 1)
    keep = jnp.logical_not(jnp.logical_and(ri < CHUNK, ci >= CHUNK))
    w_stack = [jnp.concatenate([jnp.where(keep, gw_ref[2 * j], 0.0), jnp.where(keep, gw_ref[2 * j + 1], 0.0)],
                               axis=0).astype(BF16) for j in range(GMLP_GROUPS // 2)]

    col = lax.broadcasted_iota(jnp.int32, abias_ref.shape, 1)
    first_bias = abias_ref[...] + jnp.where(jnp.logical_and(col < blk, t == 0), NEG, 0.0)

    def mixer_pieces(j):
        r0 = j * blk
        win = pl.ds(r0, 2 * blk)
        bias = first_bias if j == 0 else abias_ref[...]

        def attn_half_fn(q, swapped):
            k_ref, v_ref = (kws_scr, vws_scr) if swapped else (kw_scr, vw_scr)
            return _swa_half(q, k_ref[win, :], v_ref[win, :], sink_ref, bias, swapped)

        gmlp_fn = lambda u, v: _gmlp_block(u, v, w_stack, gb_ref[...], ln_g[...], ln_b[...])
        ret_fn = lambda q, k, v, g: _retention_block(
            q, k, v, g, cos_ref[pl.ds(r0, blk), :], sin_ref[pl.ds(r0, blk), :], st_scr[...].astype(BF16),
            d4_ref[...], xi_ref[...], zeta_ref[...], bd_ref[...], g_ret[...])
        return _mixer_pieces(y_cur, mix_scr, r0, blk, attn_half_fn, gmlp_fn, ret_fn, g_na[...], g_nb[...])

    def advance_state(kv):
        state = st_scr[...]
        st_scr[...] = jnp.where(live, state * dec_ref[...] + kv * bd_ref[...], state)

    assert tile // blk == 2 and len(FFN_CHUNKS) == 4
    attn_plain0, attn_swapped0, gmlp0, retention0 = mixer_pieces(0)
    attn_plain1, attn_swapped1, gmlp1, retention1 = mixer_pieces(1)
    gate_up = lambda c: _gate_up(h2, c, wg_ref, wu_ref)
    down = lambda f, act, c: _down(f, act, c, wd_ref)

    y_next[:, 0:OFF_QC] = _dot(h, w_in_ref[:, 0:OFF_QC])
    x1 = xres_ref[...] + _rms(merged, g_post_mix[...])
    h2 = _rms(x1, g_pre_ffn[...]).astype(BF16)
    act = gate_up(0)
    attn_plain0()
    attn_plain1()
    f = down(None, act, 0)
    act = gate_up(1)
    attn_swapped0()
    attn_swapped1()
    gmlp0()
    f = down(f, act, 1)
    act = gate_up(2)
    advance_state(retention0())
    gmlp1()
    f = down(f, act, 2)
    advance_state(retention1())
    _store_diag_blocks(so_ref, st_scr[...])
    act = gate_up(3)
    f = down(f, act, 3)
    y_next[:, OFF_QC:IN_WIDTH] = _dot(h, w_in_ref[:, OFF_QC:IN_WIDTH])
    xo_ref[...] = x1 + _rms(f, g_post_ffn[...])


def _sample_kernel(sink_ref, x_ref, cos_ref, sin_ref, ck_ref, cv_ref, st0_ref,
                   w_in_ref, w_out_ref, wg_ref, wu_ref, wd_ref,
                   g_pre_mix, g_post_mix, g_pre_ffn, g_post_ffn, g_na, g_nb, g_ret, ln_g, ln_b,
                   gw_ref, gb_ref, d4_ref, xi_ref, zeta_ref, dec_ref, bd_ref,
                   xo_ref, ko_ref, vo_ref, so_ref, gv_ref,
                   y_scr, mix_scr, kw_scr, kws_scr, vw_scr, vws_scr):
    seq = CHUNK
    nseq = x_ref.shape[0] // seq

    x = x_ref[...]
    h = _rms(x, g_pre_mix[...]).astype(BF16)
    y_scr[...] = _dot(h, w_in_ref[...])

    w_stack = [jnp.concatenate([gw_ref[2 * j, 0:seq, 0:seq], gw_ref[2 * j + 1, 0:seq, 0:seq]],
                               axis=0).astype(BF16) for j in range(GMLP_GROUPS // 2)]

    for s in range(nseq):
        r0 = s * seq
        sl = pl.ds(r0, seq)
        k_new = y_scr[sl, OFF_KA:OFF_KA + SWA_KV]
        v_new = y_scr[sl, OFF_VA:OFF_VA + SWA_KV]
        ko_ref[s] = k_new
        vo_ref[s] = v_new
        k_all = jnp.concatenate([ck_ref[s], k_new], axis=0)
        v_all = jnp.concatenate([cv_ref[s], v_new], axis=0)
        kw_scr[s] = k_all.astype(BF16)
        kws_scr[s] = pltpu.roll(k_all, HEAD_DIM, 1).astype(BF16)
        vw_scr[s] = _with_ones(v_all.astype(BF16))
        vws_scr[s] = _with_ones(pltpu.roll(v_all, HEAD_DIM, 1).astype(BF16))

        st0 = st0_ref[s]
        zero = jnp.zeros((HEAD_DIM, HEAD_DIM), F32)
        state = jnp.concatenate(
            [jnp.concatenate([st0[h] if g == h else zero for g in range(RET_HEADS)], axis=1)
             for h in range(RET_HEADS)], axis=0)

        def attn_half_fn(q, swapped):
            k_ref, v_ref = (kws_scr, vws_scr) if swapped else (kw_scr, vw_scr)
            return _swa_half(q, k_ref[s], v_ref[s], sink_ref, None, swapped)

        gmlp_fn = lambda u, v: _gmlp_block(u, v, w_stack, gb_ref[...], ln_g[...], ln_b[...])
        ret_fn = lambda q, k, v, g: _retention_block(
            q, k, v, g, cos_ref[...], sin_ref[...], state.astype(BF16),
            d4_ref[...], xi_ref[...], zeta_ref[...], bd_ref[...], g_ret[...])
        attn_plain, attn_swapped, gmlp, retention = _mixer_pieces(y_scr, mix_scr, r0, seq, attn_half_fn,
                                                                  gmlp_fn, ret_fn, g_na[...], g_nb[...])
        attn_plain()
        attn_swapped()
        gv_ref[s] = gmlp()
        _store_diag_blocks(so_ref.at[s], state * dec_ref[...] + retention())

    xo_ref[...] = _ffn_tail(x, mix_scr[...], w_out_ref, wg_ref, wu_ref, wd_ref,
                            g_post_mix[...], g_pre_ffn[...], g_post_ffn[...])


def _rotary_tables(pos):
    half = HEAD_DIM // 2
    inv = ROPE_BASE ** (-np.arange(half, dtype=np.float64) / half)
    ang = np.asarray(pos, np.float64)[:, None] * inv[None, :]
    cos = np.cos(ang)
    sin = np.sin(ang)
    cos_t = np.tile(np.concatenate([cos, cos], axis=1), (1, RET_HEADS))
    sin_t = np.tile(np.concatenate([-sin, sin], axis=1), (1, RET_HEADS))
    return jnp.asarray(cos_t, F32), jnp.asarray(sin_t, F32)


def _retention_tables(block):
    logg = np.log(1.0 - 2.0 ** (-5.0 - np.arange(RET_HEADS, dtype=np.float64)))
    idx = np.arange(block, dtype=np.float64)
    rel = idx[:, None] - idx[None, :]
    decay = np.where(rel >= 0, np.exp(logg[:, None, None] * np.maximum(rel, 0.0)), 0.0)
    d4 = decay.reshape(RET_HEADS * block, block)
    xi = np.exp(logg[:, None] * (idx + 1.0))
    zeta = np.exp(logg[:, None] * (block - 1.0 - idx))
    xi_tab = np.repeat(xi.T, HEAD_DIM, axis=1)
    zeta_tab = np.repeat(zeta.T, HEAD_DIM, axis=1)
    chunk_decay = np.repeat(np.exp(logg * block), HEAD_DIM)
    hid = np.arange(RET_WIDTH) // HEAD_DIM
    bd = (hid[:, None] == hid[None, :]).astype(np.float64)
    dec_tab = bd * chunk_decay[:, None]
    return tuple(jnp.asarray(a, F32) for a in (d4, xi_tab, zeta_tab, dec_tab, bd))


def _attn_bias(block):
    r = np.arange(block)[:, None]
    c = np.arange(2 * block)[None, :]
    visible = np.where(r < CHUNK, c < 2 * block - CHUNK, c >= CHUNK)
    return jnp.asarray(np.tile(np.where(visible, 0.0, NEG), (4, 1)), F32)


def _layer_weight(shape, layer):
    return pl.BlockSpec((None,) + shape, lambda *_: (layer,) + (0,) * len(shape), pipeline_mode=pl.Buffered(1))


def _const(shape):
    return pl.BlockSpec(shape, lambda *_: (0,) * len(shape), pipeline_mode=pl.Buffered(1))


def _layer_vec(width, layer):
    return pl.BlockSpec((None, 1, width), lambda *_: (layer, 0, 0), pipeline_mode=pl.Buffered(1))


def _weight_specs(layer):
    return [
        _layer_weight((D_MODEL, IN_WIDTH), layer),
        _layer_weight((D_MODEL, D_MODEL), layer),
        _layer_weight((D_MODEL, D_FF), layer),
        _layer_weight((D_MODEL, D_FF), layer),
        _layer_weight((D_FF, D_MODEL), layer),
        _layer_vec(D_MODEL, layer), _layer_vec(D_MODEL, layer), _layer_vec(D_MODEL, layer),
        _layer_vec(D_MODEL, layer),
        _layer_vec(SWA_Q, layer), _layer_vec(GMLP_WIDTH, layer), _layer_vec(RET_WIDTH, layer),
        _layer_vec(GMLP_WIDTH, layer), _layer_vec(GMLP_WIDTH, layer),
        _layer_weight((GMLP_GROUPS, GMLP_BLOCK, GMLP_BLOCK), layer),
    ]


def _prompt_layer(layer, x, sinks, cos_t, sin_t, weights, gb_tab, abias, ret_tabs):
    batch, seq, _ = x.shape
    tile, blk = PROMPT_TILE, PROMPT_BLOCK
    d4, xi_tab, zeta_tab, dec_tab, bd = ret_tabs
    tiles_per_seq = seq // tile
    n_tiles = batch * tiles_per_seq
    stage = lambda lag: (lambda s: jnp.clip(s - lag, 0, n_tiles - 1))
    proj_tile, mix_tile, ffn_tile = stage(0), stage(1), stage(2)
    row_block = lambda tile_of: (lambda s: (tile_of(s) // tiles_per_seq, tile_of(s) % tiles_per_seq, 0))
    in_specs = [
        pl.BlockSpec(memory_space=pltpu.SMEM),
        pl.BlockSpec((None, tile, D_MODEL), row_block(proj_tile)),
        pl.BlockSpec((None, tile, D_MODEL), row_block(ffn_tile)),
        pl.BlockSpec((tile, RET_WIDTH), lambda s: (mix_tile(s) % tiles_per_seq, 0)),
        pl.BlockSpec((tile, RET_WIDTH), lambda s: (mix_tile(s) % tiles_per_seq, 0)),
        *_weight_specs(layer),
        pl.BlockSpec((None, blk, GMLP_WIDTH), lambda s: (layer, 0, 0), pipeline_mode=pl.Buffered(1)),
        _const(abias.shape), _const(d4.shape), _const(xi_tab.shape), _const(zeta_tab.shape),
        _const(dec_tab.shape), _const(bd.shape),
    ]
    out_shape = [
        jax.ShapeDtypeStruct((batch, seq, D_MODEL), F32),
        jax.ShapeDtypeStruct((batch, WINDOW, SWA_KV), F32),
        jax.ShapeDtypeStruct((batch, WINDOW, SWA_KV), F32),
        jax.ShapeDtypeStruct((batch, RET_HEADS, HEAD_DIM, HEAD_DIM), F32),
    ]
    out_specs = [
        pl.BlockSpec((None, tile, D_MODEL), row_block(ffn_tile)),
        pl.BlockSpec((None, WINDOW, SWA_KV), lambda s: (mix_tile(s) // tiles_per_seq, 0, 0)),
        pl.BlockSpec((None, WINDOW, SWA_KV), lambda s: (mix_tile(s) // tiles_per_seq, 0, 0)),
        pl.BlockSpec((None, RET_HEADS, HEAD_DIM, HEAD_DIM), lambda s: (mix_tile(s) // tiles_per_seq, 0, 0, 0)),
    ]
    scratch = [
        pltpu.VMEM((2, tile, IN_WIDTH), F32),
        pltpu.VMEM((tile, D_MODEL), BF16),
        pltpu.VMEM((tile + blk, SWA_KV), BF16), pltpu.VMEM((tile + blk, SWA_KV), BF16),
        pltpu.VMEM((tile + blk, 2 * SWA_KV), BF16), pltpu.VMEM((tile + blk, 2 * SWA_KV), BF16),
        pltpu.VMEM((RET_WIDTH, RET_WIDTH), F32),
    ]
    return pl.pallas_call(
        functools.partial(_prompt_kernel, tiles_per_seq, n_tiles),
        grid=(n_tiles + 2,),
        in_specs=in_specs, out_specs=out_specs, out_shape=out_shape, scratch_shapes=scratch,
        compiler_params=pltpu.CompilerParams(dimension_semantics=("arbitrary",),
                                             vmem_limit_bytes=VMEM_LIMIT_BYTES),
        name=f"prompt_layer{layer}",
    )(sinks, x, x, cos_t, sin_t, *weights, gb_tab, abias, d4, xi_tab, zeta_tab, dec_tab, bd)


def _sample_layer(layer, x, sinks, cos_t, sin_t, cache_k, cache_v, state0, weights, gb_tab, ret_tabs):
    rows = x.shape[0]
    nseq = SAMPLE_SEQS
    tile = nseq * CHUNK
    n_all = rows // CHUNK
    d4, xi_tab, zeta_tab, dec_tab, bd = ret_tabs
    in_specs = [
        pl.BlockSpec(memory_space=pltpu.SMEM),
        pl.BlockSpec((tile, D_MODEL), lambda i: (i, 0)),
        _const(cos_t.shape), _const(sin_t.shape),
        pl.BlockSpec((nseq, None, WINDOW, SWA_KV), lambda i: (i, layer, 0, 0)),
        pl.BlockSpec((nseq, None, WINDOW, SWA_KV), lambda i: (i, layer, 0, 0)),
        pl.BlockSpec((nseq, None, RET_HEADS, HEAD_DIM, HEAD_DIM), lambda i: (i, layer, 0, 0, 0)),
        *_weight_specs(layer),
        pl.BlockSpec((None, CHUNK, GMLP_WIDTH), lambda i: (layer, 0, 0), pipeline_mode=pl.Buffered(1)),
        _const(d4.shape), _const(xi_tab.shape), _const(zeta_tab.shape), _const(dec_tab.shape), _const(bd.shape),
    ]
    out_shape = [
        jax.ShapeDtypeStruct((rows, D_MODEL), F32),
        jax.ShapeDtypeStruct((n_all, CHUNK, SWA_KV), F32),
        jax.ShapeDtypeStruct((n_all, CHUNK, SWA_KV), F32),
        jax.ShapeDtypeStruct((n_all, RET_HEADS, HEAD_DIM, HEAD_DIM), F32),
        jax.ShapeDtypeStruct((n_all, CHUNK, GMLP_WIDTH), F32),
    ]
    out_specs = [
        pl.BlockSpec((tile, D_MODEL), lambda i: (i, 0)),
        pl.BlockSpec((nseq, CHUNK, SWA_KV), lambda i: (i, 0, 0)),
        pl.BlockSpec((nseq, CHUNK, SWA_KV), lambda i: (i, 0, 0)),
        pl.BlockSpec((nseq, RET_HEADS, HEAD_DIM, HEAD_DIM), lambda i: (i, 0, 0, 0)),
        pl.BlockSpec((nseq, CHUNK, GMLP_WIDTH), lambda i: (i, 0, 0)),
    ]
    win = WINDOW + CHUNK
    scratch = [
        pltpu.VMEM((tile, IN_WIDTH), F32),
        pltpu.VMEM((tile, D_MODEL), BF16),
        pltpu.VMEM((nseq, win, SWA_KV), BF16), pltpu.VMEM((nseq, win, SWA_KV), BF16),
        pltpu.VMEM((nseq, win, 2 * SWA_KV), BF16), pltpu.VMEM((nseq, win, 2 * SWA_KV), BF16),
    ]
    return pl.pallas_call(
        _sample_kernel,
        grid=(n_all // nseq,),
        in_specs=in_specs, out_specs=out_specs, out_shape=out_shape, scratch_shapes=scratch,
        compiler_params=pltpu.CompilerParams(dimension_semantics=("arbitrary",),
                                             vmem_limit_bytes=VMEM_LIMIT_BYTES),
        name=f"sample_layer{layer}",
    )(sinks, x, cos_t, sin_t, cache_k, cache_v, state0, *weights, gb_tab, d4, xi_tab, zeta_tab, dec_tab, bd)


def kernel(x_prompt, x_sample, cache_swa_k, cache_swa_v, state_ret, w_in, w_out, swa_sinks,
           gmlp_w, gmlp_b, gmlp_ln_g, gmlp_ln_b, norm_a_g, norm_b_g, ret_norm_g,
           ln_pre_mix, ln_post_mix, ln_pre_ffn, ln_post_ffn, w_gate, w_up, w_down):
    batch, seq, _ = x_prompt.shape
    dec_batch, dec_seq, _ = x_sample.shape
    assert dec_seq == CHUNK and seq % PROMPT_TILE == 0 and dec_batch % SAMPLE_SEQS == 0
    assert cache_swa_k.shape[2] == WINDOW

    vec = lambda a: a.reshape(DEPTH, 1, a.shape[-1])
    weights = [
        w_in.astype(BF16), w_out.astype(BF16), w_gate.astype(BF16), w_up.astype(BF16), w_down.astype(BF16),
        vec(ln_pre_mix), vec(ln_post_mix), vec(ln_pre_ffn), vec(ln_post_ffn),
        vec(norm_a_g), vec(norm_b_g), vec(ret_norm_g), vec(gmlp_ln_g), vec(gmlp_ln_b),
        gmlp_w,
    ]
    gb_tab = jnp.repeat(jnp.swapaxes(gmlp_b, 1, 2), HEAD_DIM, axis=2)

    cos_p, sin_p = _rotary_tables(np.arange(seq))
    cos_s, sin_s = _rotary_tables(PAST_LEN + np.arange(dec_seq))
    tabs_p = _retention_tables(PROMPT_BLOCK)
    tabs_s = _retention_tables(CHUNK)
    abias = _attn_bias(PROMPT_BLOCK)

    xs = x_sample.reshape(dec_batch * dec_seq, D_MODEL)
    ck = cache_swa_k.reshape(dec_batch, DEPTH, WINDOW, SWA_KV)
    cv = cache_swa_v.reshape(dec_batch, DEPTH, WINDOW, SWA_KV)

    xp = x_prompt
    kp_l, vp_l, ks_l, vs_l, rp_l, rs_l, gv_l = [], [], [], [], [], [], []
    for layer in range(DEPTH):
        sinks = swa_sinks[layer]
        xp, kp, vp, rp = _prompt_layer(layer, xp, sinks, cos_p, sin_p, weights, gb_tab, abias, tabs_p)
        xs, ks, vs, rs, gv = _sample_layer(layer, xs, sinks, cos_s, sin_s, ck, cv, state_ret, weights,
                                           gb_tab, tabs_s)
        kp_l.append(kp); vp_l.append(vp); ks_l.append(ks); vs_l.append(vs)
        rp_l.append(rp); rs_l.append(rs); gv_l.append(gv)

    kv5 = lambda a: jnp.stack(a, axis=1).reshape(a[0].shape[0], DEPTH, a[0].shape[1], SWA_KV_HEADS, HEAD_DIM)
    return (xp, xs.reshape(dec_batch, dec_seq, D_MODEL), kv5(kp_l), kv5(vp_l), kv5(ks_l), kv5(vs_l),
            jnp.stack(rp_l, axis=1), jnp.stack(rs_l, axis=1), jnp.stack(gv_l, axis=1))
```

```python
import collections
import functools
import math

import numpy as np
import jax
import jax.numpy as jnp
from jax import lax
from jax.experimental import pallas as pl
from jax.experimental.pallas import tpu as pltpu

D_MODEL = 1024
DEPTH = 4
PAST_LEN = 2048
CHUNK = 64
HEAD_DIM = 64
SWA_HEADS = 8
SWA_KV_HEADS = 2
WINDOW = 128
SWA_Q = SWA_HEADS * HEAD_DIM
SWA_KV = SWA_KV_HEADS * HEAD_DIM
GMLP_GROUPS = 4
GMLP_BLOCK = 128
GMLP_WIDTH = GMLP_GROUPS * HEAD_DIM
RET_HEADS = 4
RET_WIDTH = RET_HEADS * HEAD_DIM
ROPE_BASE = 10000.0
IN_WIDTH = SWA_Q + 2 * SWA_KV + 2 * GMLP_WIDTH + 4 * RET_WIDTH
D_FF = 2816
EPS = 1e-6
NEG = -1e30

OFF_QA, OFF_KA, OFF_VA, OFF_UB, OFF_VB, OFF_QC, OFF_KC, OFF_VC, OFF_GC = (
    0, 512, 640, 768, 1024, 1280, 1536, 1792, 2048)

LANES = 128
PROMPT_BLOCK = 128
PROMPT_TILE = 256
SAMPLE_SEQS = 4
FFN_CHUNKS = ((0, 768), (768, 1536), (1536, 2304), (2304, D_FF))
VMEM_LIMIT_BYTES = 56 * 1024 * 1024

HEADS_PLAIN = (0, 2, 5, 7)
HEADS_SWAPPED = (1, 3, 4, 6)

F32 = jnp.float32
BF16 = jnp.bfloat16
LOG2_E = 1.0 / math.log(2.0)


def _dot(a, b):
    return jnp.dot(a, b, preferred_element_type=F32)


def _dot_nt(a, b):
    return lax.dot_general(a, b, (((1,), (1,)), ((), ())), preferred_element_type=F32)


def _rms(x, g):
    return x * lax.rsqrt(jnp.mean(x * x, axis=-1, keepdims=True) + EPS) * g


def _gelu(x):
    c = math.sqrt(2.0 / math.pi)
    return x * (0.5 * (1.0 + jnp.tanh(c * (x + 0.044715 * (x * x * x)))))


def _silu(x):
    return x / (1.0 + jnp.exp2(x * (-LOG2_E)))


def _swa_half(q, k, v_ones, sink_ref, bias, swapped):
    rows = q.shape[0]
    lo = lax.broadcasted_iota(jnp.int32, (rows, LANES), 1) < HEAD_DIM
    scale = HEAD_DIM ** -0.5 * LOG2_E

    def half(pair, keep_lo):
        qp = q[:, pair * LANES:(pair + 1) * LANES] * scale
        return jnp.where(lo if keep_lo else jnp.logical_not(lo), qp, 0.0).astype(BF16)

    lhs = jnp.concatenate([half(0, not swapped), half(1, not swapped), half(2, swapped), half(3, swapped)], axis=0)
    heads = HEADS_SWAPPED if swapped else HEADS_PLAIN
    s = _dot_nt(lhs, k)
    if bias is not None:
        s = s + bias
    ps, sink_ps = [], []
    for g, h in enumerate(heads):
        sh = s[g * rows:(g + 1) * rows]
        sink = sink_ref[h] * LOG2_E
        m = jnp.maximum(jnp.max(sh, axis=-1, keepdims=True), sink)
        ps.append(jnp.exp2(sh - m).astype(BF16))
        sink_ps.append(jnp.exp2(sink - m))
    o = _dot(jnp.concatenate(ps, axis=0), v_ones)
    return jnp.concatenate([o[g * rows:(g + 1) * rows, :LANES] / (o[g * rows:(g + 1) * rows, LANES:] + sink_ps[g])
                            for g in range(len(heads))], axis=0)


def _with_ones(v):
    return jnp.concatenate([v, jnp.ones(v.shape, v.dtype)], axis=1)


def _swa_merge(o_p, o_s):
    r = o_p.shape[0] // 4
    lo = lax.broadcasted_iota(jnp.int32, (r, LANES), 1) < HEAD_DIM
    pairs = [
        jnp.where(lo, o_p[0:r], o_s[0:r]),
        jnp.where(lo, o_p[r:2 * r], o_s[r:2 * r]),
        jnp.where(lo, o_s[2 * r:3 * r], o_p[2 * r:3 * r]),
        jnp.where(lo, o_s[3 * r:4 * r], o_p[3 * r:4 * r]),
    ]
    return jnp.concatenate(pairs, axis=1)


def _gmlp_block(u_raw, v_raw, w_stack, bias_tab, ln_g, ln_b):
    rows = u_raw.shape[0]
    u = _gelu(u_raw)
    v = _gelu(v_raw)
    mu = jnp.mean(v, axis=-1, keepdims=True)
    d = v - mu
    var = jnp.mean(d * d, axis=-1, keepdims=True)
    vn = d * lax.rsqrt(var + EPS) * ln_g + ln_b
    lo = lax.broadcasted_iota(jnp.int32, (rows, LANES), 1) < HEAD_DIM
    vb = vn.astype(BF16)
    mixed = []
    for j in range(GMLP_GROUPS // 2):
        res = _dot(w_stack[j], vb[:, j * LANES:(j + 1) * LANES])
        mixed.append(jnp.where(lo, res[0:rows], res[rows:2 * rows]))
    mixed = jnp.concatenate(mixed, axis=1) + bias_tab
    return u * mixed, vn


def _rotary(x, cos_t, sin_t):
    rows = x.shape[0]
    first = (lax.broadcasted_iota(jnp.int32, (rows, LANES), 1) % HEAD_DIM) < (HEAD_DIM // 2)
    out = []
    for j in range(RET_WIDTH // LANES):
        sl = slice(j * LANES, (j + 1) * LANES)
        xh = x[:, sl]
        partner = jnp.where(first, pltpu.roll(xh, LANES - HEAD_DIM // 2, 1), pltpu.roll(xh, HEAD_DIM // 2, 1))
        out.append(xh * cos_t[:, sl] + partner * sin_t[:, sl])
    return jnp.concatenate(out, axis=1)


def _retention_block(qc, kc, vc, gc, cos_t, sin_t, state_bf, d4, xi_tab, zeta_tab, bd_mask, ret_g):
    rows = qc.shape[0]
    qr = _rotary(qc, cos_t, sin_t)
    kr = _rotary(kc, cos_t, sin_t) * (HEAD_DIM ** -0.5)
    head = lax.broadcasted_iota(jnp.int32, (rows, RET_WIDTH), 1) // HEAD_DIM
    q4 = jnp.concatenate([jnp.where(head == h, qr, 0.0) for h in range(RET_HEADS)], axis=0).astype(BF16)
    vb = vc.astype(BF16)
    s4 = _dot_nt(q4, kr.astype(BF16)) * d4
    o4 = _dot(s4.astype(BF16), vb)
    inner = o4[0:rows]
    for h in range(1, RET_HEADS):
        inner = jnp.where(head == h, o4[h * rows:(h + 1) * rows], inner)
    cross = _dot(qr.astype(BF16), state_bf) * xi_tab
    r = inner + cross
    kz_t = jnp.transpose(kr * zeta_tab).astype(BF16)
    kv = _dot(kz_t, vb)
    avg = (bd_mask * (1.0 / HEAD_DIM)).astype(BF16)
    mu = _dot(r.astype(BF16), avg)
    d = r - mu
    var = _dot((d * d).astype(BF16), avg)
    yn = d * lax.rsqrt(var + EPS) * ret_g
    return yn * _silu(gc), kv


def _mixer_pieces(y_ref, mix_ref, r0, rows, attn_half_fn, gmlp_fn, ret_fn, g_na, g_nb):
    sl = pl.ds(r0, rows)
    held = {}

    def attention_plain():
        held["plain"] = attn_half_fn(y_ref[sl, OFF_QA:OFF_QA + SWA_Q], False)

    def attention_swapped():
        a = _swa_merge(held.pop("plain"), attn_half_fn(y_ref[sl, OFF_QA:OFF_QA + SWA_Q], True))
        mix_ref[sl, 0:SWA_Q] = _rms(a, g_na).astype(BF16)

    def gmlp():
        b, vn = gmlp_fn(y_ref[sl, OFF_UB:OFF_UB + GMLP_WIDTH], y_ref[sl, OFF_VB:OFF_VB + GMLP_WIDTH])
        mix_ref[sl, SWA_Q:SWA_Q + GMLP_WIDTH] = _rms(b, g_nb).astype(BF16)
        return vn

    def retention():
        c, kv = ret_fn(y_ref[sl, OFF_QC:OFF_QC + RET_WIDTH], y_ref[sl, OFF_KC:OFF_KC + RET_WIDTH],
                       y_ref[sl, OFF_VC:OFF_VC + RET_WIDTH], y_ref[sl, OFF_GC:OFF_GC + RET_WIDTH])
        mix_ref[sl, SWA_Q + GMLP_WIDTH:D_MODEL] = c.astype(BF16)
        return kv

    return attention_plain, attention_swapped, gmlp, retention


def _gate_up(h2, c, wg_ref, wu_ref):
    c0, c1 = FFN_CHUNKS[c]
    return (_silu(_dot(h2, wg_ref[:, c0:c1])) * _dot(h2, wu_ref[:, c0:c1])).astype(BF16)


def _down(f, act, c, wd_ref):
    c0, c1 = FFN_CHUNKS[c]
    part = _dot(act, wd_ref[c0:c1, :])
    return part if f is None else f + part


def _ffn_tail(x, mix, w_out_ref, wg_ref, wu_ref, wd_ref, g_post_mix, g_pre_ffn, g_post_ffn):
    x1 = x + _rms(_dot(mix, w_out_ref[...]), g_post_mix)
    h2 = _rms(x1, g_pre_ffn).astype(BF16)
    f = None
    for c in range(len(FFN_CHUNKS)):
        f = _down(f, _gate_up(h2, c, wg_ref, wu_ref), c, wd_ref)
    return x1 + _rms(f, g_post_ffn)


def _store_diag_blocks(out_ref, m):
    for h in range(RET_HEADS):
        out_ref[h] = m[h * HEAD_DIM:(h + 1) * HEAD_DIM, h * HEAD_DIM:(h + 1) * HEAD_DIM]


class _View:
    def __init__(self, ref, rows, cols):
        self.ref, self.rows, self.cols = ref, rows, cols

    def __getitem__(self, idx):
        assert idx is Ellipsis
        return self.ref[self.rows, self.cols]


GAIN_LAYOUT = (("pre_mix", D_MODEL), ("post_mix", D_MODEL), ("pre_ffn", D_MODEL), ("post_ffn", D_MODEL),
               ("norm_a", SWA_Q), ("norm_b", GMLP_WIDTH), ("ret_g", RET_WIDTH), ("ln_g", GMLP_WIDTH),
               ("ln_b", GMLP_WIDTH))
GAIN_ROWS = 16


def _gain_views(vec_ref):
    return [_View(vec_ref, slice(i, i + 1), slice(0, width)) for i, (_, width) in enumerate(GAIN_LAYOUT)]


def _table_views(rtab_ref, block):
    edges = (0, block, 2 * block, 2 * block + RET_WIDTH, 2 * block + 2 * RET_WIDTH)
    return [_View(rtab_ref, slice(a, b), slice(0, RET_WIDTH)) for a, b in zip(edges[:-1], edges[1:])]


def _prompt_kernel(tiles_per_seq, n_tiles, *refs):
    r = _PromptRefs(*refs)
    g_pre_mix, g_post_mix, g_pre_ffn, g_post_ffn = _gain_views(r.vec_ref)[:4]
    s = pl.program_id(0)

    @pl.when(s == 0)
    def _():
        r.mix_scr[...] = jnp.zeros(r.mix_scr.shape, BF16)
        h = _rms(r.x_ref[...], g_pre_mix[...]).astype(BF16)
        r.y_scr[0] = _dot(h, r.w_in_ref[...])

    @pl.when(jnp.logical_and(s >= 1, s <= n_tiles))
    def _():
        _prompt_steady_step(s, tiles_per_seq, *refs)

    @pl.when(s == n_tiles + 1)
    def _():
        r.xo_ref[...] = _ffn_tail(r.xres_ref[...], r.mix_scr[...], r.w_out_ref, r.wg_ref, r.wu_ref, r.wd_ref,
                                  g_post_mix[...], g_pre_ffn[...], g_post_ffn[...])


_PromptRefs = collections.namedtuple("_PromptRefs", [
    "sink_ref", "x_ref", "xres_ref", "cos_ref", "sin_ref", "w_in_ref", "w_out_ref", "wg_ref", "wu_ref", "wd_ref",
    "vec_ref", "gw_ref", "gb_ref", "abias_ref", "d4_ref", "rtab_ref",
    "xo_ref", "ko_ref", "vo_ref", "so_ref",
    "y_scr", "mix_scr", "kw_scr", "kws_scr", "vw_scr", "vws_scr", "st_scr"])


def _prompt_steady_step(s, tiles_per_seq,
                        sink_ref, x_ref, xres_ref, cos_ref, sin_ref, w_in_ref, w_out_ref, wg_ref, wu_ref, wd_ref,
                        vec_ref, gw_ref, gb_ref, abias_ref, d4_ref, rtab_ref,
                        xo_ref, ko_ref, vo_ref, so_ref,
                        y_scr, mix_scr, kw_scr, kws_scr, vw_scr, vws_scr, st_scr):
    tile = x_ref.shape[0]
    blk = PROMPT_BLOCK
    g_pre_mix, g_post_mix, g_pre_ffn, g_post_ffn, g_na, g_nb, g_ret, ln_g, ln_b = _gain_views(vec_ref)
    xi_ref, zeta_ref, dec_ref, bd_ref = _table_views(rtab_ref, blk)
    t = (s - 1) % tiles_per_seq

    @pl.when(t == 0)
    def _():
        for ref in (kw_scr, kws_scr, vw_scr, vws_scr):
            ref[0:blk, :] = jnp.zeros((blk, ref.shape[1]), BF16)
        st_scr[...] = jnp.zeros(st_scr.shape, F32)

    @pl.when(t > 0)
    def _():
        for ref in (kw_scr, kws_scr, vw_scr, vws_scr):
            ref[0:blk, :] = ref[tile:tile + blk, :]

    y_cur = y_scr.at[(s + 1) % 2]
    y_next = y_scr.at[s % 2]

    merged = _dot(mix_scr[...], w_out_ref[...])

    h = _rms(x_ref[...], g_pre_mix[...]).astype(BF16)

    k_new = y_cur[:, OFF_KA:OFF_KA + SWA_KV]
    v_new = y_cur[:, OFF_VA:OFF_VA + SWA_KV]
    kw_scr[blk:blk + tile, :] = k_new.astype(BF16)
    kws_scr[blk:blk + tile, :] = pltpu.roll(k_new, HEAD_DIM, 1).astype(BF16)
    vw_scr[blk:blk + tile, :] = _with_ones(v_new.astype(BF16))
    vws_scr[blk:blk + tile, :] = _with_ones(pltpu.roll(v_new, HEAD_DIM, 1).astype(BF16))
    ko_ref[...] = k_new[tile - WINDOW:tile]
    vo_ref[...] = v_new[tile - WINDOW:tile]

    ri = lax.broadcasted_iota(jnp.int32, (blk, blk), 0)
    ci = lax.broadcasted_iota(jnp.int32, (blk, blk), 1)
    keep = jnp.logical_not(jnp.logical_and(ri < CHUNK, ci >= CHUNK))
    w_stack = [jnp.concatenate([jnp.where(keep, gw_ref[2 * j], 0.0), jnp.where(keep, gw_ref[2 * j + 1], 0.0)],
                               axis=0).astype(BF16) for j in range(GMLP_GROUPS // 2)]

    col = lax.broadcasted_iota(jnp.int32, abias_ref.shape, 1)
    first_bias = abias_ref[...] + jnp.where(jnp.logical_and(col < blk, t == 0), NEG, 0.0)

    def mixer_pieces(j):
        r0 = j * blk
        win = pl.ds(r0, 2 * blk)
        bias = first_bias if j == 0 else abias_ref[...]

        def attn_half_fn(q, swapped):
            k_ref, v_ref = (kws_scr, vws_scr) if swapped else (kw_scr, vw_scr)
            return _swa_half(q, k_ref[win, :], v_ref[win, :], sink_ref, bias, swapped)

        gmlp_fn = lambda u, v: _gmlp_block(u, v, w_stack, gb_ref[...], ln_g[...], ln_b[...])
        ret_fn = lambda q, k, v, g: _retention_block(
            q, k, v, g, cos_ref[pl.ds(r0, blk), :], sin_ref[pl.ds(r0, blk), :], st_scr[...].astype(BF16),
            d4_ref[...], xi_ref[...], zeta_ref[...], bd_ref[...], g_ret[...])
        return _mixer_pieces(y_cur, mix_scr, r0, blk, attn_half_fn, gmlp_fn, ret_fn, g_na[...], g_nb[...])

    def advance_state(kv):
        st_scr[...] = st_scr[...] * dec_ref[...] + kv * bd_ref[...]

    assert tile // blk == 2 and len(FFN_CHUNKS) == 4
    attn_plain0, attn_swapped0, gmlp0, retention0 = mixer_pieces(0)
    attn_plain1, attn_swapped1, gmlp1, retention1 = mixer_pieces(1)
    gate_up = lambda c: _gate_up(h2, c, wg_ref, wu_ref)
    down = lambda f, act, c: _down(f, act, c, wd_ref)

    y_next[:, 0:OFF_QC] = _dot(h, w_in_ref[:, 0:OFF_QC])
    x1 = xres_ref[...] + _rms(merged, g_post_mix[...])
    h2 = _rms(x1, g_pre_ffn[...]).astype(BF16)
    act = gate_up(0)
    attn_plain0()
    attn_plain1()
    f = down(None, act, 0)
    act = gate_up(1)
    attn_swapped0()
    attn_swapped1()
    gmlp0()
    f = down(f, act, 1)
    act = gate_up(2)
    advance_state(retention0())
    gmlp1()
    f = down(f, act, 2)
    advance_state(retention1())
    _store_diag_blocks(so_ref, st_scr[...])
    act = gate_up(3)
    f = down(f, act, 3)
    y_next[:, OFF_QC:IN_WIDTH] = _dot(h, w_in_ref[:, OFF_QC:IN_WIDTH])
    xo_ref[...] = x1 + _rms(f, g_post_ffn[...])


def _sample_kernel(sink_ref, x_ref, cos_ref, sin_ref, ck_ref, cv_ref, st0_ref,
                   w_in_ref, w_out_ref, wg_ref, wu_ref, wd_ref,
                   vec_ref, gw_ref, gb_ref, d4_ref, rtab_ref,
                   xo_ref, ko_ref, vo_ref, so_ref, gv_ref,
                   y_scr, mix_scr, kw_scr, kws_scr, vw_scr, vws_scr):
    seq = CHUNK
    nseq = x_ref.shape[0] // seq
    g_pre_mix, g_post_mix, g_pre_ffn, g_post_ffn, g_na, g_nb, g_ret, ln_g, ln_b = _gain_views(vec_ref)
    xi_ref, zeta_ref, dec_ref, bd_ref = _table_views(rtab_ref, seq)

    x = x_ref[...]
    h = _rms(x, g_pre_mix[...]).astype(BF16)
    y_scr[...] = _dot(h, w_in_ref[...])

    w_stack = [jnp.concatenate([gw_ref[2 * j, 0:seq, 0:seq], gw_ref[2 * j + 1, 0:seq, 0:seq]],
                               axis=0).astype(BF16) for j in range(GMLP_GROUPS // 2)]

    for s in range(nseq):
        r0 = s * seq
        sl = pl.ds(r0, seq)
        k_new = y_scr[sl, OFF_KA:OFF_KA + SWA_KV]
        v_new = y_scr[sl, OFF_VA:OFF_VA + SWA_KV]
        ko_ref[s] = k_new
        vo_ref[s] = v_new
        k_all = jnp.concatenate([ck_ref[s], k_new], axis=0)
        v_all = jnp.concatenate([cv_ref[s], v_new], axis=0)
        kw_scr[s] = k_all.astype(BF16)
        kws_scr[s] = pltpu.roll(k_all, HEAD_DIM, 1).astype(BF16)
        vw_scr[s] = _with_ones(v_all.astype(BF16))
        vws_scr[s] = _with_ones(pltpu.roll(v_all, HEAD_DIM, 1).astype(BF16))

        st0 = st0_ref[s]
        zero = jnp.zeros((HEAD_DIM, HEAD_DIM), F32)
        state = jnp.concatenate(
            [jnp.concatenate([st0[h] if g == h else zero for g in range(RET_HEADS)], axis=1)
             for h in range(RET_HEADS)], axis=0)

        def attn_half_fn(q, swapped):
            k_ref, v_ref = (kws_scr, vws_scr) if swapped else (kw_scr, vw_scr)
            return _swa_half(q, k_ref[s], v_ref[s], sink_ref, None, swapped)

        gmlp_fn = lambda u, v: _gmlp_block(u, v, w_stack, gb_ref[...], ln_g[...], ln_b[...])
        ret_fn = lambda q, k, v, g: _retention_block(
            q, k, v, g, cos_ref[...], sin_ref[...], state.astype(BF16),
            d4_ref[...], xi_ref[...], zeta_ref[...], bd_ref[...], g_ret[...])
        attn_plain, attn_swapped, gmlp, retention = _mixer_pieces(y_scr, mix_scr, r0, seq, attn_half_fn,
                                                                  gmlp_fn, ret_fn, g_na[...], g_nb[...])
        attn_plain()
        attn_swapped()
        gv_ref[s] = gmlp()
        _store_diag_blocks(so_ref.at[s], state * dec_ref[...] + retention())

    xo_ref[...] = _ffn_tail(x, mix_scr[...], w_out_ref, wg_ref, wu_ref, wd_ref,
                            g_post_mix[...], g_pre_ffn[...], g_post_ffn[...])


def _rotary_tables(pos):
    half = HEAD_DIM // 2
    inv = ROPE_BASE ** (-np.arange(half, dtype=np.float64) / half)
    ang = np.asarray(pos, np.float64)[:, None] * inv[None, :]
    cos = np.cos(ang)
    sin = np.sin(ang)
    cos_t = np.tile(np.concatenate([cos, cos], axis=1), (1, RET_HEADS))
    sin_t = np.tile(np.concatenate([-sin, sin], axis=1), (1, RET_HEADS))
    return jnp.asarray(cos_t, F32), jnp.asarray(sin_t, F32)


def _retention_tables(block):
    logg = np.log(1.0 - 2.0 ** (-5.0 - np.arange(RET_HEADS, dtype=np.float64)))
    idx = np.arange(block, dtype=np.float64)
    rel = idx[:, None] - idx[None, :]
    decay = np.where(rel >= 0, np.exp(logg[:, None, None] * np.maximum(rel, 0.0)), 0.0)
    d4 = decay.reshape(RET_HEADS * block, block)
    xi = np.exp(logg[:, None] * (idx + 1.0))
    zeta = np.exp(logg[:, None] * (block - 1.0 - idx))
    xi_tab = np.repeat(xi.T, HEAD_DIM, axis=1)
    zeta_tab = np.repeat(zeta.T, HEAD_DIM, axis=1)
    chunk_decay = np.repeat(np.exp(logg * block), HEAD_DIM)
    hid = np.arange(RET_WIDTH) // HEAD_DIM
    bd = (hid[:, None] == hid[None, :]).astype(np.float64)
    dec_tab = bd * chunk_decay[:, None]
    return jnp.asarray(d4, F32), jnp.asarray(np.concatenate([xi_tab, zeta_tab, dec_tab, bd], axis=0), F32)


def _attn_bias(block):
    r = np.arange(block)[:, None]
    c = np.arange(2 * block)[None, :]
    visible = np.where(r < CHUNK, c < 2 * block - CHUNK, c >= CHUNK)
    return jnp.asarray(np.tile(np.where(visible, 0.0, NEG), (4, 1)), F32)


def _layer_weight(shape, layer):
    return pl.BlockSpec((None,) + shape, lambda *_: (layer,) + (0,) * len(shape), pipeline_mode=pl.Buffered(1))


def _const(shape):
    return pl.BlockSpec(shape, lambda *_: (0,) * len(shape), pipeline_mode=pl.Buffered(1))


def _weight_specs(layer):
    return [
        _layer_weight((D_MODEL, IN_WIDTH), layer),
        _layer_weight((D_MODEL, D_MODEL), layer),
        _layer_weight((D_MODEL, D_FF), layer),
        _layer_weight((D_MODEL, D_FF), layer),
        _layer_weight((D_FF, D_MODEL), layer),
        _layer_weight((GAIN_ROWS, D_MODEL), layer),
        _layer_weight((GMLP_GROUPS, GMLP_BLOCK, GMLP_BLOCK), layer),
    ]


def _prompt_layer(layer, x, sinks, cos_t, sin_t, weights, gb_tab, abias, ret_tabs):
    batch, seq, _ = x.shape
    tile, blk = PROMPT_TILE, PROMPT_BLOCK
    d4, rtab = ret_tabs
    tiles_per_seq = seq // tile
    n_tiles = batch * tiles_per_seq
    stage = lambda lag: (lambda s: jnp.clip(s - lag, 0, n_tiles - 1))
    proj_tile, mix_tile, ffn_tile = stage(0), stage(1), stage(2)
    row_block = lambda tile_of: (lambda s: (tile_of(s) // tiles_per_seq, tile_of(s) % tiles_per_seq, 0))
    in_specs = [
        pl.BlockSpec(memory_space=pltpu.SMEM),
        pl.BlockSpec((None, tile, D_MODEL), row_block(proj_tile)),
        pl.BlockSpec((None, tile, D_MODEL), row_block(ffn_tile)),
        pl.BlockSpec((tile, RET_WIDTH), lambda s: (mix_tile(s) % tiles_per_seq, 0)),
        pl.BlockSpec((tile, RET_WIDTH), lambda s: (mix_tile(s) % tiles_per_seq, 0)),
        *_weight_specs(layer),
        pl.BlockSpec((None, blk, GMLP_WIDTH), lambda s: (layer, 0, 0), pipeline_mode=pl.Buffered(1)),
        _const(abias.shape), _const(d4.shape), _const(rtab.shape),
    ]
    out_shape = [
        jax.ShapeDtypeStruct((batch, seq, D_MODEL), F32),
        jax.ShapeDtypeStruct((batch, WINDOW, SWA_KV), F32),
        jax.ShapeDtypeStruct((batch, WINDOW, SWA_KV), F32),
        jax.ShapeDtypeStruct((batch, RET_HEADS, HEAD_DIM, HEAD_DIM), F32),
    ]
    out_specs = [
        pl.BlockSpec((None, tile, D_MODEL), row_block(ffn_tile)),
        pl.BlockSpec((None, WINDOW, SWA_KV), lambda s: (mix_tile(s) // tiles_per_seq, 0, 0)),
        pl.BlockSpec((None, WINDOW, SWA_KV), lambda s: (mix_tile(s) // tiles_per_seq, 0, 0)),
        pl.BlockSpec((None, RET_HEADS, HEAD_DIM, HEAD_DIM), lambda s: (mix_tile(s) // tiles_per_seq, 0, 0, 0)),
    ]
    scratch = [
        pltpu.VMEM((2, tile, IN_WIDTH), F32),
        pltpu.VMEM((tile, D_MODEL), BF16),
        pltpu.VMEM((tile + blk, SWA_KV), BF16), pltpu.VMEM((tile + blk, SWA_KV), BF16),
        pltpu.VMEM((tile + blk, 2 * SWA_KV), BF16), pltpu.VMEM((tile + blk, 2 * SWA_KV), BF16),
        pltpu.VMEM((RET_WIDTH, RET_WIDTH), F32),
    ]
    return pl.pallas_call(
        functools.partial(_prompt_kernel, tiles_per_seq, n_tiles),
        grid=(n_tiles + 2,),
        in_specs=in_specs, out_specs=out_specs, out_shape=out_shape, scratch_shapes=scratch,
        compiler_params=pltpu.CompilerParams(dimension_semantics=("arbitrary",),
                                             vmem_limit_bytes=VMEM_LIMIT_BYTES),
        name=f"prompt_layer{layer}",
    )(sinks, x, x, cos_t, sin_t, *weights, gb_tab, abias, d4, rtab)


def _sample_layer(layer, x, sinks, cos_t, sin_t, cache_k, cache_v, state0, weights, gb_tab, ret_tabs):
    rows = x.shape[0]
    nseq = SAMPLE_SEQS
    tile = nseq * CHUNK
    n_all = rows // CHUNK
    d4, rtab = ret_tabs
    in_specs = [
        pl.BlockSpec(memory_space=pltpu.SMEM),
        pl.BlockSpec((tile, D_MODEL), lambda i: (i, 0)),
        _const(cos_t.shape), _const(sin_t.shape),
        pl.BlockSpec((nseq, None, WINDOW, SWA_KV), lambda i: (i, layer, 0, 0)),
        pl.BlockSpec((nseq, None, WINDOW, SWA_KV), lambda i: (i, layer, 0, 0)),
        pl.BlockSpec((nseq, None, RET_HEADS, HEAD_DIM, HEAD_DIM), lambda i: (i, layer, 0, 0, 0)),
        *_weight_specs(layer),
        pl.BlockSpec((None, CHUNK, GMLP_WIDTH), lambda i: (layer, 0, 0), pipeline_mode=pl.Buffered(1)),
        _const(d4.shape), _const(rtab.shape),
    ]
    out_shape = [
        jax.ShapeDtypeStruct((rows, D_MODEL), F32),
        jax.ShapeDtypeStruct((n_all, CHUNK, SWA_KV), F32),
        jax.ShapeDtypeStruct((n_all, CHUNK, SWA_KV), F32),
        jax.ShapeDtypeStruct((n_all, RET_HEADS, HEAD_DIM, HEAD_DIM), F32),
        jax.ShapeDtypeStruct((n_all, CHUNK, GMLP_WIDTH), F32),
    ]
    out_specs = [
        pl.BlockSpec((tile, D_MODEL), lambda i: (i, 0)),
        pl.BlockSpec((nseq, CHUNK, SWA_KV), lambda i: (i, 0, 0)),
        pl.BlockSpec((nseq, CHUNK, SWA_KV), lambda i: (i, 0, 0)),
        pl.BlockSpec((nseq, RET_HEADS, HEAD_DIM, HEAD_DIM), lambda i: (i, 0, 0, 0)),
        pl.BlockSpec((nseq, CHUNK, GMLP_WIDTH), lambda i: (i, 0, 0)),
    ]
    win = WINDOW + CHUNK
    scratch = [
        pltpu.VMEM((tile, IN_WIDTH), F32),
        pltpu.VMEM((tile, D_MODEL), BF16),
        pltpu.VMEM((nseq, win, SWA_KV), BF16), pltpu.VMEM((nseq, win, SWA_KV), BF16),
        pltpu.VMEM((nseq, win, 2 * SWA_KV), BF16), pltpu.VMEM((nseq, win, 2 * SWA_KV), BF16),
    ]
    return pl.pallas_call(
        _sample_kernel,
        grid=(n_all // nseq,),
        in_specs=in_specs, out_specs=out_specs, out_shape=out_shape, scratch_shapes=scratch,
        compiler_params=pltpu.CompilerParams(dimension_semantics=("arbitrary",),
                                             vmem_limit_bytes=VMEM_LIMIT_BYTES),
        name=f"sample_layer{layer}",
    )(sinks, x, cos_t, sin_t, cache_k, cache_v, state0, *weights, gb_tab, d4, rtab)


def kernel(x_prompt, x_sample, cache_swa_k, cache_swa_v, state_ret, w_in, w_out, swa_sinks,
           gmlp_w, gmlp_b, gmlp_ln_g, gmlp_ln_b, norm_a_g, norm_b_g, ret_norm_g,
           ln_pre_mix, ln_post_mix, ln_pre_ffn, ln_post_ffn, w_gate, w_up, w_down):
    batch, seq, _ = x_prompt.shape
    dec_batch, dec_seq, _ = x_sample.shape
    assert dec_seq == CHUNK and seq % PROMPT_TILE == 0 and dec_batch % SAMPLE_SEQS == 0
    assert cache_swa_k.shape[2] == WINDOW

    gains = (ln_pre_mix, ln_post_mix, ln_pre_ffn, ln_post_ffn, norm_a_g, norm_b_g, ret_norm_g, gmlp_ln_g,
             gmlp_ln_b)
    assert all(g.shape[-1] == width for g, (_, width) in zip(gains, GAIN_LAYOUT))
    vecs = jnp.stack([jnp.pad(g, ((0, 0), (0, D_MODEL - g.shape[-1]))) for g in gains], axis=1)
    vecs = jnp.pad(vecs, ((0, 0), (0, GAIN_ROWS - len(gains)), (0, 0)))
    weights = [
        w_in.astype(BF16), w_out.astype(BF16), w_gate.astype(BF16), w_up.astype(BF16), w_down.astype(BF16),
        vecs, gmlp_w,
    ]
    gb_tab = jnp.repeat(jnp.swapaxes(gmlp_b, 1, 2), HEAD_DIM, axis=2)

    cos_p, sin_p = _rotary_tables(np.arange(seq))
    cos_s, sin_s = _rotary_tables(PAST_LEN + np.arange(dec_seq))
    tabs_p = _retention_tables(PROMPT_BLOCK)
    tabs_s = _retention_tables(CHUNK)
    abias = _attn_bias(PROMPT_BLOCK)

    xs = x_sample.reshape(dec_batch * dec_seq, D_MODEL)
    ck = cache_swa_k.reshape(dec_batch, DEPTH, WINDOW, SWA_KV)
    cv = cache_swa_v.reshape(dec_batch, DEPTH, WINDOW, SWA_KV)

    xp = x_prompt
    kp_l, vp_l, ks_l, vs_l, rp_l, rs_l, gv_l = [], [], [], [], [], [], []
    for layer in range(DEPTH):
        sinks = swa_sinks[layer]
        xp, kp, vp, rp = _prompt_layer(layer, xp, sinks, cos_p, sin_p, weights, gb_tab, abias, tabs_p)
        xs, ks, vs, rs, gv = _sample_layer(layer, xs, sinks, cos_s, sin_s, ck, cv, state_ret, weights,
                                           gb_tab, tabs_s)
        kp_l.append(kp); vp_l.append(vp); ks_l.append(ks); vs_l.append(vs)
        rp_l.append(rp); rs_l.append(rs); gv_l.append(gv)

    kv5 = lambda a: jnp.stack(a, axis=1).reshape(a[0].shape[0], DEPTH, a[0].shape[1], SWA_KV_HEADS, HEAD_DIM)
    return (xp, xs.reshape(dec_batch, dec_seq, D_MODEL), kv5(kp_l), kv5(vp_l), kv5(ks_l), kv5(vs_l),
            jnp.stack(rp_l, axis=1), jnp.stack(rs_l, axis=1), jnp.stack(gv_l, axis=1))
```

```python
import collections
import functools
import math

import numpy as np
import jax
import jax.numpy as jnp
from jax import lax
from jax.experimental import pallas as pl
from jax.experimental.pallas import tpu as pltpu

D_MODEL = 1024
DEPTH = 4
PAST_LEN = 2048
CHUNK = 64
HEAD_DIM = 64
SWA_HEADS = 8
SWA_KV_HEADS = 2
WINDOW = 128
SWA_Q = SWA_HEADS * HEAD_DIM
SWA_KV = SWA_KV_HEADS * HEAD_DIM
GMLP_GROUPS = 4
GMLP_BLOCK = 128
GMLP_WIDTH = GMLP_GROUPS * HEAD_DIM
RET_HEADS = 4
RET_WIDTH = RET_HEADS * HEAD_DIM
ROPE_BASE = 10000.0
IN_WIDTH = SWA_Q + 2 * SWA_KV + 2 * GMLP_WIDTH + 4 * RET_WIDTH
D_FF = 2816
EPS = 1e-6
NEG = -1e30

OFF_QA, OFF_KA, OFF_VA, OFF_UB, OFF_VB, OFF_QC, OFF_KC, OFF_VC, OFF_GC = (
    0, 512, 640, 768, 1024, 1280, 1536, 1792, 2048)

LANES = 128
PROMPT_BLOCK = 128
PROMPT_TILE = 256
SAMPLE_SEQS = 8
FFN_CHUNKS = ((0, 768), (768, 1536), (1536, 2304), (2304, D_FF))
VMEM_LIMIT_BYTES = 56 * 1024 * 1024

HEADS_PLAIN = (0, 2, 5, 7)
HEADS_SWAPPED = (1, 3, 4, 6)

F32 = jnp.float32
BF16 = jnp.bfloat16
LOG2_E = 1.0 / math.log(2.0)


def _dot(a, b):
    return jnp.dot(a, b, preferred_element_type=F32)


def _dot_nt(a, b):
    return lax.dot_general(a, b, (((1,), (1,)), ((), ())), preferred_element_type=F32)


def _rms(x, g):
    return x * lax.rsqrt(jnp.mean(x * x, axis=-1, keepdims=True) + EPS) * g


def _gelu(x):
    c = math.sqrt(2.0 / math.pi)
    return x * (0.5 * (1.0 + jnp.tanh(c * (x + 0.044715 * (x * x * x)))))


def _silu(x):
    return x / (1.0 + jnp.exp2(x * (-LOG2_E)))


def _swa_half(q, k, v_ones, sink_ref, bias, swapped):
    rows = q.shape[0]
    lo = lax.broadcasted_iota(jnp.int32, (rows, LANES), 1) < HEAD_DIM
    scale = HEAD_DIM ** -0.5 * LOG2_E

    def half(pair, keep_lo):
        qp = q[:, pair * LANES:(pair + 1) * LANES] * scale
        return jnp.where(lo if keep_lo else jnp.logical_not(lo), qp, 0.0).astype(BF16)

    lhs = jnp.concatenate([half(0, not swapped), half(1, not swapped), half(2, swapped), half(3, swapped)], axis=0)
    heads = HEADS_SWAPPED if swapped else HEADS_PLAIN
    s = _dot_nt(lhs, k)
    if bias is not None:
        s = s + bias
    ps, sink_ps = [], []
    for g, h in enumerate(heads):
        sh = s[g * rows:(g + 1) * rows]
        sink = sink_ref[h] * LOG2_E
        m = jnp.maximum(jnp.max(sh, axis=-1, keepdims=True), sink)
        ps.append(jnp.exp2(sh - m).astype(BF16))
        sink_ps.append(jnp.exp2(sink - m))
    o = _dot(jnp.concatenate(ps, axis=0), v_ones)
    return jnp.concatenate([o[g * rows:(g + 1) * rows, :LANES] / (o[g * rows:(g + 1) * rows, LANES:] + sink_ps[g])
                            for g in range(len(heads))], axis=0)


def _with_ones(v):
    return jnp.concatenate([v, jnp.ones(v.shape, v.dtype)], axis=1)


def _swa_merge(o_p, o_s):
    r = o_p.shape[0] // 4
    lo = lax.broadcasted_iota(jnp.int32, (r, LANES), 1) < HEAD_DIM
    pairs = [
        jnp.where(lo, o_p[0:r], o_s[0:r]),
        jnp.where(lo, o_p[r:2 * r], o_s[r:2 * r]),
        jnp.where(lo, o_s[2 * r:3 * r], o_p[2 * r:3 * r]),
        jnp.where(lo, o_s[3 * r:4 * r], o_p[3 * r:4 * r]),
    ]
    return jnp.concatenate(pairs, axis=1)


def _gmlp_block(u_raw, v_raw, w_stack, bias_tab, ln_g, ln_b):
    rows = u_raw.shape[0]
    u = _gelu(u_raw)
    v = _gelu(v_raw)
    mu = jnp.mean(v, axis=-1, keepdims=True)
    d = v - mu
    var = jnp.mean(d * d, axis=-1, keepdims=True)
    vn = d * lax.rsqrt(var + EPS) * ln_g + ln_b
    lo = lax.broadcasted_iota(jnp.int32, (rows, LANES), 1) < HEAD_DIM
    vb = vn.astype(BF16)
    mixed = []
    for j in range(GMLP_GROUPS // 2):
        res = _dot(w_stack[j], vb[:, j * LANES:(j + 1) * LANES])
        mixed.append(jnp.where(lo, res[0:rows], res[rows:2 * rows]))
    mixed = jnp.concatenate(mixed, axis=1) + bias_tab
    return u * mixed, vn


def _rotary(x, cos_t, sin_t):
    rows = x.shape[0]
    first = (lax.broadcasted_iota(jnp.int32, (rows, LANES), 1) % HEAD_DIM) < (HEAD_DIM // 2)
    out = []
    for j in range(RET_WIDTH // LANES):
        sl = slice(j * LANES, (j + 1) * LANES)
        xh = x[:, sl]
        partner = jnp.where(first, pltpu.roll(xh, LANES - HEAD_DIM // 2, 1), pltpu.roll(xh, HEAD_DIM // 2, 1))
        out.append(xh * cos_t[:, sl] + partner * sin_t[:, sl])
    return jnp.concatenate(out, axis=1)


def _retention_block(qc, kc, vc, gc, cos_t, sin_t, state_bf, d4, xi_tab, zeta_tab, bd_mask, ret_g):
    rows = qc.shape[0]
    qr = _rotary(qc, cos_t, sin_t)
    kr = _rotary(kc, cos_t, sin_t) * (HEAD_DIM ** -0.5)
    head = lax.broadcasted_iota(jnp.int32, (rows, RET_WIDTH), 1) // HEAD_DIM
    q4 = jnp.concatenate([jnp.where(head == h, qr, 0.0) for h in range(RET_HEADS)], axis=0).astype(BF16)
    vb = vc.astype(BF16)
    s4 = _dot_nt(q4, kr.astype(BF16)) * d4
    o4 = _dot(s4.astype(BF16), vb)
    inner = o4[0:rows]
    for h in range(1, RET_HEADS):
        inner = jnp.where(head == h, o4[h * rows:(h + 1) * rows], inner)
    cross = _dot(qr.astype(BF16), state_bf) * xi_tab
    r = inner + cross
    kz_t = jnp.transpose(kr * zeta_tab).astype(BF16)
    kv = _dot(kz_t, vb)
    avg = (bd_mask * (1.0 / HEAD_DIM)).astype(BF16)
    mu = _dot(r.astype(BF16), avg)
    d = r - mu
    var = _dot((d * d).astype(BF16), avg)
    yn = d * lax.rsqrt(var + EPS) * ret_g
    return yn * _silu(gc), kv


def _mixer_pieces(y_ref, mix_ref, r0, rows, attn_half_fn, gmlp_fn, ret_fn, g_na, g_nb):
    sl = pl.ds(r0, rows)
    held = {}

    def attention_plain():
        held["plain"] = attn_half_fn(y_ref[sl, OFF_QA:OFF_QA + SWA_Q], False)

    def attention_swapped():
        a = _swa_merge(held.pop("plain"), attn_half_fn(y_ref[sl, OFF_QA:OFF_QA + SWA_Q], True))
        mix_ref[sl, 0:SWA_Q] = _rms(a, g_na).astype(BF16)

    def gmlp():
        b, vn = gmlp_fn(y_ref[sl, OFF_UB:OFF_UB + GMLP_WIDTH], y_ref[sl, OFF_VB:OFF_VB + GMLP_WIDTH])
        mix_ref[sl, SWA_Q:SWA_Q + GMLP_WIDTH] = _rms(b, g_nb).astype(BF16)
        return vn

    def retention():
        c, kv = ret_fn(y_ref[sl, OFF_QC:OFF_QC + RET_WIDTH], y_ref[sl, OFF_KC:OFF_KC + RET_WIDTH],
                       y_ref[sl, OFF_VC:OFF_VC + RET_WIDTH], y_ref[sl, OFF_GC:OFF_GC + RET_WIDTH])
        mix_ref[sl, SWA_Q + GMLP_WIDTH:D_MODEL] = c.astype(BF16)
        return kv

    return attention_plain, attention_swapped, gmlp, retention


def _gate_up(h2, c, wg_ref, wu_ref):
    c0, c1 = FFN_CHUNKS[c]
    return (_silu(_dot(h2, wg_ref[:, c0:c1])) * _dot(h2, wu_ref[:, c0:c1])).astype(BF16)


def _down(f, act, c, wd_ref):
    c0, c1 = FFN_CHUNKS[c]
    part = _dot(act, wd_ref[c0:c1, :])
    return part if f is None else f + part


def _ffn_tail(x, mix, w_out_ref, wg_ref, wu_ref, wd_ref, g_post_mix, g_pre_ffn, g_post_ffn):
    x1 = x + _rms(_dot(mix, w_out_ref[...]), g_post_mix)
    h2 = _rms(x1, g_pre_ffn).astype(BF16)
    f = None
    for c in range(len(FFN_CHUNKS)):
        f = _down(f, _gate_up(h2, c, wg_ref, wu_ref), c, wd_ref)
    return x1 + _rms(f, g_post_ffn)


def _store_diag_blocks(out_ref, m):
    for h in range(RET_HEADS):
        out_ref[h] = m[h * HEAD_DIM:(h + 1) * HEAD_DIM, h * HEAD_DIM:(h + 1) * HEAD_DIM]


class _View:
    def __init__(self, ref, rows, cols):
        self.ref, self.rows, self.cols = ref, rows, cols

    def __getitem__(self, idx):
        assert idx is Ellipsis
        return self.ref[self.rows, self.cols]


GAIN_LAYOUT = (("pre_mix", D_MODEL), ("post_mix", D_MODEL), ("pre_ffn", D_MODEL), ("post_ffn", D_MODEL),
               ("norm_a", SWA_Q), ("norm_b", GMLP_WIDTH), ("ret_g", RET_WIDTH), ("ln_g", GMLP_WIDTH),
               ("ln_b", GMLP_WIDTH))
GAIN_ROWS = 16


def _gain_views(vec_ref):
    return [_View(vec_ref, slice(i, i + 1), slice(0, width)) for i, (_, width) in enumerate(GAIN_LAYOUT)]


def _table_views(rtab_ref, block):
    edges = (0, block, 2 * block, 2 * block + RET_WIDTH, 2 * block + 2 * RET_WIDTH)
    return [_View(rtab_ref, slice(a, b), slice(0, RET_WIDTH)) for a, b in zip(edges[:-1], edges[1:])]


def _prompt_kernel(tiles_per_seq, n_tiles, *refs):
    r = _PromptRefs(*refs)
    g_pre_mix, g_post_mix, g_pre_ffn, g_post_ffn = _gain_views(r.vec_ref)[:4]
    s = pl.program_id(0)

    @pl.when(s == 0)
    def _():
        r.mix_scr[...] = jnp.zeros(r.mix_scr.shape, BF16)
        h = _rms(r.x_ref[...], g_pre_mix[...]).astype(BF16)
        r.y_scr[0] = _dot(h, r.w_in_ref[...])

    @pl.when(jnp.logical_and(s >= 1, s <= n_tiles))
    def _():
        _prompt_steady_step(s, tiles_per_seq, *refs)

    @pl.when(s == n_tiles + 1)
    def _():
        r.xo_ref[...] = _ffn_tail(r.xres_ref[...], r.mix_scr[...], r.w_out_ref, r.wg_ref, r.wu_ref, r.wd_ref,
                                  g_post_mix[...], g_pre_ffn[...], g_post_ffn[...])


_PromptRefs = collections.namedtuple("_PromptRefs", [
    "sink_ref", "x_ref", "xres_ref", "cos_ref", "sin_ref", "w_in_ref", "w_out_ref", "wg_ref", "wu_ref", "wd_ref",
    "vec_ref", "gw_ref", "gb_ref", "abias_ref", "d4_ref", "rtab_ref",
    "xo_ref", "ko_ref", "vo_ref", "so_ref",
    "y_scr", "mix_scr", "kw_scr", "kws_scr", "vw_scr", "vws_scr", "st_scr"])


def _prompt_steady_step(s, tiles_per_seq,
                        sink_ref, x_ref, xres_ref, cos_ref, sin_ref, w_in_ref, w_out_ref, wg_ref, wu_ref, wd_ref,
                        vec_ref, gw_ref, gb_ref, abias_ref, d4_ref, rtab_ref,
                        xo_ref, ko_ref, vo_ref, so_ref,
                        y_scr, mix_scr, kw_scr, kws_scr, vw_scr, vws_scr, st_scr):
    tile = x_ref.shape[0]
    blk = PROMPT_BLOCK
    g_pre_mix, g_post_mix, g_pre_ffn, g_post_ffn, g_na, g_nb, g_ret, ln_g, ln_b = _gain_views(vec_ref)
    xi_ref, zeta_ref, dec_ref, bd_ref = _table_views(rtab_ref, blk)
    t = (s - 1) % tiles_per_seq

    @pl.when(t == 0)
    def _():
        for ref in (kw_scr, kws_scr, vw_scr, vws_scr):
            ref[0:blk, :] = jnp.zeros((blk, ref.shape[1]), BF16)
        st_scr[...] = jnp.zeros(st_scr.shape, F32)

    @pl.when(t > 0)
    def _():
        for ref in (kw_scr, kws_scr, vw_scr, vws_scr):
            ref[0:blk, :] = ref[tile:tile + blk, :]

    y_cur = y_scr.at[(s + 1) % 2]
    y_next = y_scr.at[s % 2]

    merged = _dot(mix_scr[...], w_out_ref[...])

    h = _rms(x_ref[...], g_pre_mix[...]).astype(BF16)

    k_new = y_cur[:, OFF_KA:OFF_KA + SWA_KV]
    v_new = y_cur[:, OFF_VA:OFF_VA + SWA_KV]
    kw_scr[blk:blk + tile, :] = k_new.astype(BF16)
    kws_scr[blk:blk + tile, :] = pltpu.roll(k_new, HEAD_DIM, 1).astype(BF16)
    vw_scr[blk:blk + tile, :] = _with_ones(v_new.astype(BF16))
    vws_scr[blk:blk + tile, :] = _with_ones(pltpu.roll(v_new, HEAD_DIM, 1).astype(BF16))
    ko_ref[...] = k_new[tile - WINDOW:tile]
    vo_ref[...] = v_new[tile - WINDOW:tile]

    ri = lax.broadcasted_iota(jnp.int32, (blk, blk), 0)
    ci = lax.broadcasted_iota(jnp.int32, (blk, blk), 1)
    keep = jnp.logical_not(jnp.logical_and(ri < CHUNK, ci >= CHUNK))
    w_stack = [jnp.concatenate([jnp.where(keep, gw_ref[2 * j], 0.0), jnp.where(keep, gw_ref[2 * j + 1], 0.0)],
                               axis=0).astype(BF16) for j in range(GMLP_GROUPS // 2)]

    col = lax.broadcasted_iota(jnp.int32, abias_ref.shape, 1)
    first_bias = abias_ref[...] + jnp.where(jnp.logical_and(col < blk, t == 0), NEG, 0.0)

    def mixer_pieces(j):
        r0 = j * blk
        win = pl.ds(r0, 2 * blk)
        bias = first_bias if j == 0 else abias_ref[...]

        def attn_half_fn(q, swapped):
            k_ref, v_ref = (kws_scr, vws_scr) if swapped else (kw_scr, vw_scr)
            return _swa_half(q, k_ref[win, :], v_ref[win, :], sink_ref, bias, swapped)

        gmlp_fn = lambda u, v: _gmlp_block(u, v, w_stack, gb_ref[...], ln_g[...], ln_b[...])
        ret_fn = lambda q, k, v, g: _retention_block(
            q, k, v, g, cos_ref[pl.ds(r0, blk), :], sin_ref[pl.ds(r0, blk), :], st_scr[...].astype(BF16),
            d4_ref[...], xi_ref[...], zeta_ref[...], bd_ref[...], g_ret[...])
        return _mixer_pieces(y_cur, mix_scr, r0, blk, attn_half_fn, gmlp_fn, ret_fn, g_na[...], g_nb[...])

    def advance_state(kv):
        st_scr[...] = st_scr[...] * dec_ref[...] + kv * bd_ref[...]

    assert tile // blk == 2 and len(FFN_CHUNKS) == 4
    attn_plain0, attn_swapped0, gmlp0, retention0 = mixer_pieces(0)
    attn_plain1, attn_swapped1, gmlp1, retention1 = mixer_pieces(1)
    gate_up = lambda c: _gate_up(h2, c, wg_ref, wu_ref)
    down = lambda f, act, c: _down(f, act, c, wd_ref)

    y_next[:, 0:OFF_QC] = _dot(h, w_in_ref[:, 0:OFF_QC])
    x1 = xres_ref[...] + _rms(merged, g_post_mix[...])
    h2 = _rms(x1, g_pre_ffn[...]).astype(BF16)
    act = gate_up(0)
    attn_plain0()
    f = down(None, act, 0)
    attn_swapped0()
    act = gate_up(1)
    attn_plain1()
    f = down(f, act, 1)
    attn_swapped1()
    act = gate_up(2)
    gmlp0()
    f = down(f, act, 2)
    gmlp1()
    act = gate_up(3)
    advance_state(retention0())
    advance_state(retention1())
    _store_diag_blocks(so_ref, st_scr[...])
    f = down(f, act, 3)
    y_next[:, OFF_QC:IN_WIDTH] = _dot(h, w_in_ref[:, OFF_QC:IN_WIDTH])
    xo_ref[...] = x1 + _rms(f, g_post_ffn[...])


def _sample_kernel(sink_ref, x_ref, cos_ref, sin_ref, ck_ref, cv_ref, st0_ref,
                   w_in_ref, w_out_ref, wg_ref, wu_ref, wd_ref,
                   vec_ref, gw_ref, gb_ref, d4_ref, rtab_ref,
                   xo_ref, ko_ref, vo_ref, so_ref, gv_ref,
                   y_scr, mix_scr, kw_scr, kws_scr, vw_scr, vws_scr):
    seq = CHUNK
    nseq = x_ref.shape[0] // seq
    g_pre_mix, g_post_mix, g_pre_ffn, g_post_ffn, g_na, g_nb, g_ret, ln_g, ln_b = _gain_views(vec_ref)
    xi_ref, zeta_ref, dec_ref, bd_ref = _table_views(rtab_ref, seq)

    x = x_ref[...]
    h = _rms(x, g_pre_mix[...]).astype(BF16)
    y_scr[...] = _dot(h, w_in_ref[...])

    w_stack = [jnp.concatenate([gw_ref[2 * j, 0:seq, 0:seq], gw_ref[2 * j + 1, 0:seq, 0:seq]],
                               axis=0).astype(BF16) for j in range(GMLP_GROUPS // 2)]

    for s in range(nseq):
        r0 = s * seq
        sl = pl.ds(r0, seq)
        k_new = y_scr[sl, OFF_KA:OFF_KA + SWA_KV]
        v_new = y_scr[sl, OFF_VA:OFF_VA + SWA_KV]
        ko_ref[s] = k_new
        vo_ref[s] = v_new
        k_all = jnp.concatenate([ck_ref[s], k_new], axis=0)
        v_all = jnp.concatenate([cv_ref[s], v_new], axis=0)
        kw_scr[s] = k_all.astype(BF16)
        kws_scr[s] = pltpu.roll(k_all, HEAD_DIM, 1).astype(BF16)
        vw_scr[s] = _with_ones(v_all.astype(BF16))
        vws_scr[s] = _with_ones(pltpu.roll(v_all, HEAD_DIM, 1).astype(BF16))

        st0 = st0_ref[s]
        zero = jnp.zeros((HEAD_DIM, HEAD_DIM), F32)
        state = jnp.concatenate(
            [jnp.concatenate([st0[h] if g == h else zero for g in range(RET_HEADS)], axis=1)
             for h in range(RET_HEADS)], axis=0)

        def attn_half_fn(q, swapped):
            k_ref, v_ref = (kws_scr, vws_scr) if swapped else (kw_scr, vw_scr)
            return _swa_half(q, k_ref[s], v_ref[s], sink_ref, None, swapped)

        gmlp_fn = lambda u, v: _gmlp_block(u, v, w_stack, gb_ref[...], ln_g[...], ln_b[...])
        ret_fn = lambda q, k, v, g: _retention_block(
            q, k, v, g, cos_ref[...], sin_ref[...], state.astype(BF16),
            d4_ref[...], xi_ref[...], zeta_ref[...], bd_ref[...], g_ret[...])
        attn_plain, attn_swapped, gmlp, retention = _mixer_pieces(y_scr, mix_scr, r0, seq, attn_half_fn,
                                                                  gmlp_fn, ret_fn, g_na[...], g_nb[...])
        attn_plain()
        attn_swapped()
        gv_ref[s] = gmlp()
        _store_diag_blocks(so_ref.at[s], state * dec_ref[...] + retention())

    xo_ref[...] = _ffn_tail(x, mix_scr[...], w_out_ref, wg_ref, wu_ref, wd_ref,
                            g_post_mix[...], g_pre_ffn[...], g_post_ffn[...])


def _rotary_tables(pos):
    half = HEAD_DIM // 2
    inv = ROPE_BASE ** (-np.arange(half, dtype=np.float64) / half)
    ang = np.asarray(pos, np.float64)[:, None] * inv[None, :]
    cos = np.cos(ang)
    sin = np.sin(ang)
    cos_t = np.tile(np.concatenate([cos, cos], axis=1), (1, RET_HEADS))
    sin_t = np.tile(np.concatenate([-sin, sin], axis=1), (1, RET_HEADS))
    return jnp.asarray(cos_t, F32), jnp.asarray(sin_t, F32)


def _retention_tables(block):
    logg = np.log(1.0 - 2.0 ** (-5.0 - np.arange(RET_HEADS, dtype=np.float64)))
    idx = np.arange(block, dtype=np.float64)
    rel = idx[:, None] - idx[None, :]
    decay = np.where(rel >= 0, np.exp(logg[:, None, None] * np.maximum(rel, 0.0)), 0.0)
    d4 = decay.reshape(RET_HEADS * block, block)
    xi = np.exp(logg[:, None] * (idx + 1.0))
    zeta = np.exp(logg[:, None] * (block - 1.0 - idx))
    xi_tab = np.repeat(xi.T, HEAD_DIM, axis=1)
    zeta_tab = np.repeat(zeta.T, HEAD_DIM, axis=1)
    chunk_decay = np.repeat(np.exp(logg * block), HEAD_DIM)
    hid = np.arange(RET_WIDTH) // HEAD_DIM
    bd = (hid[:, None] == hid[None, :]).astype(np.float64)
    dec_tab = bd * chunk_decay[:, None]
    return jnp.asarray(d4, F32), jnp.asarray(np.concatenate([xi_tab, zeta_tab, dec_tab, bd], axis=0), F32)


def _attn_bias(block):
    r = np.arange(block)[:, None]
    c = np.arange(2 * block)[None, :]
    visible = np.where(r < CHUNK, c < 2 * block - CHUNK, c >= CHUNK)
    return jnp.asarray(np.tile(np.where(visible, 0.0, NEG), (4, 1)), F32)


def _layer_weight(shape, layer):
    return pl.BlockSpec((None,) + shape, lambda *_: (layer,) + (0,) * len(shape), pipeline_mode=pl.Buffered(1))


def _const(shape):
    return pl.BlockSpec(shape, lambda *_: (0,) * len(shape), pipeline_mode=pl.Buffered(1))


def _weight_specs(layer):
    return [
        _layer_weight((D_MODEL, IN_WIDTH), layer),
        _layer_weight((D_MODEL, D_MODEL), layer),
        _layer_weight((D_MODEL, D_FF), layer),
        _layer_weight((D_MODEL, D_FF), layer),
        _layer_weight((D_FF, D_MODEL), layer),
        _layer_weight((GAIN_ROWS, D_MODEL), layer),
        _layer_weight((GMLP_GROUPS, GMLP_BLOCK, GMLP_BLOCK), layer),
    ]


def _prompt_layer(layer, x, sinks, cos_t, sin_t, weights, gb_tab, abias, ret_tabs):
    batch, seq, _ = x.shape
    tile, blk = PROMPT_TILE, PROMPT_BLOCK
    d4, rtab = ret_tabs
    tiles_per_seq = seq // tile
    n_tiles = batch * tiles_per_seq
    stage = lambda lag: (lambda s: jnp.clip(s - lag, 0, n_tiles - 1))
    proj_tile, mix_tile, ffn_tile = stage(0), stage(1), stage(2)
    row_block = lambda tile_of: (lambda s: (tile_of(s) // tiles_per_seq, tile_of(s) % tiles_per_seq, 0))
    in_specs = [
        pl.BlockSpec(memory_space=pltpu.SMEM),
        pl.BlockSpec((None, tile, D_MODEL), row_block(proj_tile)),
        pl.BlockSpec((None, tile, D_MODEL), row_block(ffn_tile)),
        pl.BlockSpec((tile, RET_WIDTH), lambda s: (mix_tile(s) % tiles_per_seq, 0)),
        pl.BlockSpec((tile, RET_WIDTH), lambda s: (mix_tile(s) % tiles_per_seq, 0)),
        *_weight_specs(layer),
        pl.BlockSpec((None, blk, GMLP_WIDTH), lambda s: (layer, 0, 0), pipeline_mode=pl.Buffered(1)),
        _const(abias.shape), _const(d4.shape), _const(rtab.shape),
    ]
    out_shape = [
        jax.ShapeDtypeStruct((batch, seq, D_MODEL), F32),
        jax.ShapeDtypeStruct((batch, WINDOW, SWA_KV), F32),
        jax.ShapeDtypeStruct((batch, WINDOW, SWA_KV), F32),
        jax.ShapeDtypeStruct((batch, RET_HEADS, HEAD_DIM, HEAD_DIM), F32),
    ]
    out_specs = [
        pl.BlockSpec((None, tile, D_MODEL), row_block(ffn_tile)),
        pl.BlockSpec((None, WINDOW, SWA_KV), lambda s: (mix_tile(s) // tiles_per_seq, 0, 0)),
        pl.BlockSpec((None, WINDOW, SWA_KV), lambda s: (mix_tile(s) // tiles_per_seq, 0, 0)),
        pl.BlockSpec((None, RET_HEADS, HEAD_DIM, HEAD_DIM), lambda s: (mix_tile(s) // tiles_per_seq, 0, 0, 0)),
    ]
    scratch = [
        pltpu.VMEM((2, tile, IN_WIDTH), F32),
        pltpu.VMEM((tile, D_MODEL), BF16),
        pltpu.VMEM((tile + blk, SWA_KV), BF16), pltpu.VMEM((tile + blk, SWA_KV), BF16),
        pltpu.VMEM((tile + blk, 2 * SWA_KV), BF16), pltpu.VMEM((tile + blk, 2 * SWA_KV), BF16),
        pltpu.VMEM((RET_WIDTH, RET_WIDTH), F32),
    ]
    return pl.pallas_call(
        functools.partial(_prompt_kernel, tiles_per_seq, n_tiles),
        grid=(n_tiles + 2,),
        in_specs=in_specs, out_specs=out_specs, out_shape=out_shape, scratch_shapes=scratch,
        compiler_params=pltpu.CompilerParams(dimension_semantics=("arbitrary",),
                                             vmem_limit_bytes=VMEM_LIMIT_BYTES),
        name=f"prompt_layer{layer}",
    )(sinks, x, x, cos_t, sin_t, *weights, gb_tab, abias, d4, rtab)


def _sample_layer(layer, x, sinks, cos_t, sin_t, cache_k, cache_v, state0, weights, gb_tab, ret_tabs):
    rows = x.shape[0]
    nseq = SAMPLE_SEQS
    tile = nseq * CHUNK
    n_all = rows // CHUNK
    d4, rtab = ret_tabs
    in_specs = [
        pl.BlockSpec(memory_space=pltpu.SMEM),
        pl.BlockSpec((tile, D_MODEL), lambda i: (i, 0)),
        _const(cos_t.shape), _const(sin_t.shape),
        pl.BlockSpec((nseq, None, WINDOW, SWA_KV), lambda i: (i, layer, 0, 0)),
        pl.BlockSpec((nseq, None, WINDOW, SWA_KV), lambda i: (i, layer, 0, 0)),
        pl.BlockSpec((nseq, None, RET_HEADS, HEAD_DIM, HEAD_DIM), lambda i: (i, layer, 0, 0, 0)),
        *_weight_specs(layer),
        pl.BlockSpec((None, CHUNK, GMLP_WIDTH), lambda i: (layer, 0, 0), pipeline_mode=pl.Buffered(1)),
        _const(d4.shape), _const(rtab.shape),
    ]
    out_shape = [
        jax.ShapeDtypeStruct((rows, D_MODEL), F32),
        jax.ShapeDtypeStruct((n_all, CHUNK, SWA_KV), F32),
        jax.ShapeDtypeStruct((n_all, CHUNK, SWA_KV), F32),
        jax.ShapeDtypeStruct((n_all, RET_HEADS, HEAD_DIM, HEAD_DIM), F32),
        jax.ShapeDtypeStruct((n_all, CHUNK, GMLP_WIDTH), F32),
    ]
    out_specs = [
        pl.BlockSpec((tile, D_MODEL), lambda i: (i, 0)),
        pl.BlockSpec((nseq, CHUNK, SWA_KV), lambda i: (i, 0, 0)),
        pl.BlockSpec((nseq, CHUNK, SWA_KV), lambda i: (i, 0, 0)),
        pl.BlockSpec((nseq, RET_HEADS, HEAD_DIM, HEAD_DIM), lambda i: (i, 0, 0, 0)),
        pl.BlockSpec((nseq, CHUNK, GMLP_WIDTH), lambda i: (i, 0, 0)),
    ]
    win = WINDOW + CHUNK
    scratch = [
        pltpu.VMEM((tile, IN_WIDTH), F32),
        pltpu.VMEM((tile, D_MODEL), BF16),
        pltpu.VMEM((nseq, win, SWA_KV), BF16), pltpu.VMEM((nseq, win, SWA_KV), BF16),
        pltpu.VMEM((nseq, win, 2 * SWA_KV), BF16), pltpu.VMEM((nseq, win, 2 * SWA_KV), BF16),
    ]
    return pl.pallas_call(
        _sample_kernel,
        grid=(n_all // nseq,),
        in_specs=in_specs, out_specs=out_specs, out_shape=out_shape, scratch_shapes=scratch,
        compiler_params=pltpu.CompilerParams(dimension_semantics=("arbitrary",),
                                             vmem_limit_bytes=VMEM_LIMIT_BYTES),
        name=f"sample_layer{layer}",
    )(sinks, x, cos_t, sin_t, cache_k, cache_v, state0, *weights, gb_tab, d4, rtab)


def kernel(x_prompt, x_sample, cache_swa_k, cache_swa_v, state_ret, w_in, w_out, swa_sinks,
           gmlp_w, gmlp_b, gmlp_ln_g, gmlp_ln_b, norm_a_g, norm_b_g, ret_norm_g,
           ln_pre_mix, ln_post_mix, ln_pre_ffn, ln_post_ffn, w_gate, w_up, w_down):
    batch, seq, _ = x_prompt.shape
    dec_batch, dec_seq, _ = x_sample.shape
    assert dec_seq == CHUNK and seq % PROMPT_TILE == 0 and dec_batch % SAMPLE_SEQS == 0
    assert cache_swa_k.shape[2] == WINDOW

    gains = (ln_pre_mix, ln_post_mix, ln_pre_ffn, ln_post_ffn, norm_a_g, norm_b_g, ret_norm_g, gmlp_ln_g,
             gmlp_ln_b)
    assert all(g.shape[-1] == width for g, (_, width) in zip(gains, GAIN_LAYOUT))
    vecs = jnp.stack([jnp.pad(g, ((0, 0), (0, D_MODEL - g.shape[-1]))) for g in gains], axis=1)
    vecs = jnp.pad(vecs, ((0, 0), (0, GAIN_ROWS - len(gains)), (0, 0)))
    weights = [
        w_in.astype(BF16), w_out.astype(BF16), w_gate.astype(BF16), w_up.astype(BF16), w_down.astype(BF16),
        vecs, gmlp_w,
    ]
    gb_tab = jnp.repeat(jnp.swapaxes(gmlp_b, 1, 2), HEAD_DIM, axis=2)

    cos_p, sin_p = _rotary_tables(np.arange(seq))
    cos_s, sin_s = _rotary_tables(PAST_LEN + np.arange(dec_seq))
    tabs_p = _retention_tables(PROMPT_BLOCK)
    tabs_s = _retention_tables(CHUNK)
    abias = _attn_bias(PROMPT_BLOCK)

    xs = x_sample.reshape(dec_batch * dec_seq, D_MODEL)
    ck = cache_swa_k.reshape(dec_batch, DEPTH, WINDOW, SWA_KV)
    cv = cache_swa_v.reshape(dec_batch, DEPTH, WINDOW, SWA_KV)

    xp = x_prompt
    kp_l, vp_l, ks_l, vs_l, rp_l, rs_l, gv_l = [], [], [], [], [], [], []
    for layer in range(DEPTH):
        sinks = swa_sinks[layer]
        xp, kp, vp, rp = _prompt_layer(layer, xp, sinks, cos_p, sin_p, weights, gb_tab, abias, tabs_p)
        xs, ks, vs, rs, gv = _sample_layer(layer, xs, sinks, cos_s, sin_s, ck, cv, state_ret, weights,
                                           gb_tab, tabs_s)
        kp_l.append(kp); vp_l.append(vp); ks_l.append(ks); vs_l.append(vs)
        rp_l.append(rp); rs_l.append(rs); gv_l.append(gv)

    kv5 = lambda a: jnp.stack(a, axis=1).reshape(a[0].shape[0], DEPTH, a[0].shape[1], SWA_KV_HEADS, HEAD_DIM)
    return (xp, xs.reshape(dec_batch, dec_seq, D_MODEL), kv5(kp_l), kv5(vp_l), kv5(ks_l), kv5(vs_l),
            jnp.stack(rp_l, axis=1), jnp.stack(rs_l, axis=1), jnp.stack(gv_l, axis=1))
```

```python
import collections
import functools
import math

import numpy as np
import jax
import jax.numpy as jnp
from jax import lax
from jax.experimental import pallas as pl
from jax.experimental.pallas import tpu as pltpu

D_MODEL = 1024
DEPTH = 4
PAST_LEN = 2048
CHUNK = 64
HEAD_DIM = 64
SWA_HEADS = 8
SWA_KV_HEADS = 2
WINDOW = 128
SWA_Q = SWA_HEADS * HEAD_DIM
SWA_KV = SWA_KV_HEADS * HEAD_DIM
GMLP_GROUPS = 4
GMLP_BLOCK = 128
GMLP_WIDTH = GMLP_GROUPS * HEAD_DIM
RET_HEADS = 4
RET_WIDTH = RET_HEADS * HEAD_DIM
ROPE_BASE = 10000.0
IN_WIDTH = SWA_Q + 2 * SWA_KV + 2 * GMLP_WIDTH + 4 * RET_WIDTH
D_FF = 2816
EPS = 1e-6
NEG = -1e30

OFF_QA, OFF_KA, OFF_VA, OFF_UB, OFF_VB, OFF_QC, OFF_KC, OFF_VC, OFF_GC = (
    0, 512, 640, 768, 1024, 1280, 1536, 1792, 2048)

LANES = 128
PROMPT_BLOCK = 128
PROMPT_TILE = 256
SAMPLE_SEQS = 8
FFN_CHUNKS = ((0, 768), (768, 1536), (1536, 2304), (2304, D_FF))
VMEM_LIMIT_BYTES = 56 * 1024 * 1024

HEADS_PLAIN = (0, 2, 5, 7)
HEADS_SWAPPED = (1, 3, 4, 6)

F32 = jnp.float32
BF16 = jnp.bfloat16
LOG2_E = 1.0 / math.log(2.0)


def _dot(a, b):
    return jnp.dot(a, b, preferred_element_type=F32)


def _dot_nt(a, b):
    return lax.dot_general(a, b, (((1,), (1,)), ((), ())), preferred_element_type=F32)


def _rms(x, g):
    return x * lax.rsqrt(jnp.mean(x * x, axis=-1, keepdims=True) + EPS) * g


def _gelu(x):
    c = math.sqrt(2.0 / math.pi)
    return x * (0.5 * (1.0 + jnp.tanh(c * (x + 0.044715 * (x * x * x)))))


def _silu(x):
    return x / (1.0 + jnp.exp2(x * (-LOG2_E)))


def _swa_half(q, k, v_ones, sink_ref, bias, swapped):
    rows = q.shape[0]
    lo = lax.broadcasted_iota(jnp.int32, (rows, LANES), 1) < HEAD_DIM
    scale = HEAD_DIM ** -0.5 * LOG2_E

    def half(pair, keep_lo):
        qp = q[:, pair * LANES:(pair + 1) * LANES] * scale
        return jnp.where(lo if keep_lo else jnp.logical_not(lo), qp, 0.0).astype(BF16)

    lhs = jnp.concatenate([half(0, not swapped), half(1, not swapped), half(2, swapped), half(3, swapped)], axis=0)
    heads = HEADS_SWAPPED if swapped else HEADS_PLAIN
    s = _dot_nt(lhs, k)
    if bias is not None:
        s = s + bias
    ps, sink_ps = [], []
    for g, h in enumerate(heads):
        sh = s[g * rows:(g + 1) * rows]
        sink = sink_ref[h] * LOG2_E
        m = jnp.maximum(jnp.max(sh, axis=-1, keepdims=True), sink)
        ps.append(jnp.exp2(sh - m).astype(BF16))
        sink_ps.append(jnp.exp2(sink - m))
    o = _dot(jnp.concatenate(ps, axis=0), v_ones)
    return jnp.concatenate([o[g * rows:(g + 1) * rows, :LANES] / (o[g * rows:(g + 1) * rows, LANES:] + sink_ps[g])
                            for g in range(len(heads))], axis=0)


def _with_ones(v):
    return jnp.concatenate([v, jnp.ones(v.shape, v.dtype)], axis=1)


def _swa_merge(o_p, o_s):
    r = o_p.shape[0] // 4
    lo = lax.broadcasted_iota(jnp.int32, (r, LANES), 1) < HEAD_DIM
    pairs = [
        jnp.where(lo, o_p[0:r], o_s[0:r]),
        jnp.where(lo, o_p[r:2 * r], o_s[r:2 * r]),
        jnp.where(lo, o_s[2 * r:3 * r], o_p[2 * r:3 * r]),
        jnp.where(lo, o_s[3 * r:4 * r], o_p[3 * r:4 * r]),
    ]
    return jnp.concatenate(pairs, axis=1)


def _gmlp_block(u_raw, v_raw, w_stack, bias_tab, ln_g, ln_b):
    rows = u_raw.shape[0]
    u = _gelu(u_raw)
    v = _gelu(v_raw)
    mu = jnp.mean(v, axis=-1, keepdims=True)
    d = v - mu
    var = jnp.mean(d * d, axis=-1, keepdims=True)
    vn = d * lax.rsqrt(var + EPS) * ln_g + ln_b
    lo = lax.broadcasted_iota(jnp.int32, (rows, LANES), 1) < HEAD_DIM
    vb = vn.astype(BF16)
    mixed = []
    for j in range(GMLP_GROUPS // 2):
        res = _dot(w_stack[j], vb[:, j * LANES:(j + 1) * LANES])
        mixed.append(jnp.where(lo, res[0:rows], res[rows:2 * rows]))
    mixed = jnp.concatenate(mixed, axis=1) + bias_tab
    return u * mixed, vn


def _rotary(x, cos_t, sin_t):
    rows = x.shape[0]
    first = (lax.broadcasted_iota(jnp.int32, (rows, LANES), 1) % HEAD_DIM) < (HEAD_DIM // 2)
    out = []
    for j in range(RET_WIDTH // LANES):
        sl = slice(j * LANES, (j + 1) * LANES)
        xh = x[:, sl]
        partner = jnp.where(first, pltpu.roll(xh, LANES - HEAD_DIM // 2, 1), pltpu.roll(xh, HEAD_DIM // 2, 1))
        out.append(xh * cos_t[:, sl] + partner * sin_t[:, sl])
    return jnp.concatenate(out, axis=1)


def _retention_block(qc, kc, vc, gc, cos_t, sin_t, state_bf, d4, xi_tab, zeta_tab, bd_mask, ret_g):
    rows = qc.shape[0]
    qr = _rotary(qc, cos_t, sin_t)
    kr = _rotary(kc, cos_t, sin_t) * (HEAD_DIM ** -0.5)
    head = lax.broadcasted_iota(jnp.int32, (rows, RET_WIDTH), 1) // HEAD_DIM
    q4 = jnp.concatenate([jnp.where(head == h, qr, 0.0) for h in range(RET_HEADS)], axis=0).astype(BF16)
    vb = vc.astype(BF16)
    s4 = _dot_nt(q4, kr.astype(BF16)) * d4
    o4 = _dot(s4.astype(BF16), vb)
    inner = o4[0:rows]
    for h in range(1, RET_HEADS):
        inner = jnp.where(head == h, o4[h * rows:(h + 1) * rows], inner)
    cross = _dot(qr.astype(BF16), state_bf) * xi_tab
    r = inner + cross
    kz_t = jnp.transpose(kr * zeta_tab).astype(BF16)
    kv = _dot(kz_t, vb)
    avg = (bd_mask * (1.0 / HEAD_DIM)).astype(BF16)
    mu = _dot(r.astype(BF16), avg)
    d = r - mu
    var = _dot((d * d).astype(BF16), avg)
    yn = d * lax.rsqrt(var + EPS) * ret_g
    return yn * _silu(gc), kv


def _mixer_pieces(y_ref, mix_ref, r0, rows, attn_half_fn, gmlp_fn, ret_fn, g_na, g_nb):
    sl = pl.ds(r0, rows)
    held = {}

    def attention_plain():
        held["plain"] = attn_half_fn(y_ref[sl, OFF_QA:OFF_QA + SWA_Q], False)

    def attention_swapped():
        a = _swa_merge(held.pop("plain"), attn_half_fn(y_ref[sl, OFF_QA:OFF_QA + SWA_Q], True))
        mix_ref[sl, 0:SWA_Q] = _rms(a, g_na).astype(BF16)

    def gmlp():
        b, vn = gmlp_fn(y_ref[sl, OFF_UB:OFF_UB + GMLP_WIDTH], y_ref[sl, OFF_VB:OFF_VB + GMLP_WIDTH])
        mix_ref[sl, SWA_Q:SWA_Q + GMLP_WIDTH] = _rms(b, g_nb).astype(BF16)
        return vn

    def retention():
        c, kv = ret_fn(y_ref[sl, OFF_QC:OFF_QC + RET_WIDTH], y_ref[sl, OFF_KC:OFF_KC + RET_WIDTH],
                       y_ref[sl, OFF_VC:OFF_VC + RET_WIDTH], y_ref[sl, OFF_GC:OFF_GC + RET_WIDTH])
        mix_ref[sl, SWA_Q + GMLP_WIDTH:D_MODEL] = c.astype(BF16)
        return kv

    return attention_plain, attention_swapped, gmlp, retention


def _gate_up(h2, c, wg_ref, wu_ref):
    c0, c1 = FFN_CHUNKS[c]
    return (_silu(_dot(h2, wg_ref[:, c0:c1])) * _dot(h2, wu_ref[:, c0:c1])).astype(BF16)


def _down(f, act, c, wd_ref):
    c0, c1 = FFN_CHUNKS[c]
    part = _dot(act, wd_ref[c0:c1, :])
    return part if f is None else f + part


def _ffn_tail(x, mix, w_out_ref, wg_ref, wu_ref, wd_ref, g_post_mix, g_pre_ffn, g_post_ffn):
    x1 = x + _rms(_dot(mix, w_out_ref[...]), g_post_mix)
    h2 = _rms(x1, g_pre_ffn).astype(BF16)
    f = None
    for c in range(len(FFN_CHUNKS)):
        f = _down(f, _gate_up(h2, c, wg_ref, wu_ref), c, wd_ref)
    return x1 + _rms(f, g_post_ffn)


def _store_diag_blocks(out_ref, m):
    for h in range(RET_HEADS):
        out_ref[h] = m[h * HEAD_DIM:(h + 1) * HEAD_DIM, h * HEAD_DIM:(h + 1) * HEAD_DIM]


class _View:
    def __init__(self, ref, rows, cols):
        self.ref, self.rows, self.cols = ref, rows, cols

    def __getitem__(self, idx):
        assert idx is Ellipsis
        return self.ref[self.rows, self.cols]


GAIN_LAYOUT = (("pre_mix", D_MODEL), ("post_mix", D_MODEL), ("pre_ffn", D_MODEL), ("post_ffn", D_MODEL),
               ("norm_a", SWA_Q), ("norm_b", GMLP_WIDTH), ("ret_g", RET_WIDTH), ("ln_g", GMLP_WIDTH),
               ("ln_b", GMLP_WIDTH))
GAIN_ROWS = 16


def _gain_views(vec_ref):
    return [_View(vec_ref, slice(i, i + 1), slice(0, width)) for i, (_, width) in enumerate(GAIN_LAYOUT)]


def _table_views(rtab_ref, block):
    edges = (0, block, 2 * block, 2 * block + RET_WIDTH, 2 * block + 2 * RET_WIDTH)
    return [_View(rtab_ref, slice(a, b), slice(0, RET_WIDTH)) for a, b in zip(edges[:-1], edges[1:])]


N_PROMPT_INPUTS = 16
N_PROMPT_OUTPUTS = 4


def _prompt_kernel(tiles_per_seq, n_tiles, n_cast, *refs):
    n_in, n_out = N_PROMPT_INPUTS, N_PROMPT_OUTPUTS
    cast_in = refs[n_in:n_in + n_cast]
    cast_out = refs[n_in + n_cast + n_out:n_in + n_out + 2 * n_cast]
    for src, dst in zip(cast_in, cast_out):
        dst[...] = src[...].astype(BF16)
    refs = refs[:n_in] + refs[n_in + n_cast:n_in + n_cast + n_out] + refs[n_in + n_out + 2 * n_cast:]

    r = _PromptRefs(*refs)
    g_pre_mix, g_post_mix, g_pre_ffn, g_post_ffn = _gain_views(r.vec_ref)[:4]
    s = pl.program_id(0)

    @pl.when(s == 0)
    def _():
        r.mix_scr[...] = jnp.zeros(r.mix_scr.shape, BF16)
        h = _rms(r.x_ref[...], g_pre_mix[...]).astype(BF16)
        r.y_scr[0] = _dot(h, r.w_in_ref[...])

    @pl.when(jnp.logical_and(s >= 1, s <= n_tiles))
    def _():
        _prompt_steady_step(s, tiles_per_seq, *refs)

    @pl.when(s == n_tiles + 1)
    def _():
        r.xo_ref[...] = _ffn_tail(r.xres_ref[...], r.mix_scr[...], r.w_out_ref, r.wg_ref, r.wu_ref, r.wd_ref,
                                  g_post_mix[...], g_pre_ffn[...], g_post_ffn[...])


_PromptRefs = collections.namedtuple("_PromptRefs", [
    "sink_ref", "x_ref", "xres_ref", "cos_ref", "sin_ref", "w_in_ref", "w_out_ref", "wg_ref", "wu_ref", "wd_ref",
    "vec_ref", "gw_ref", "gb_ref", "abias_ref", "d4_ref", "rtab_ref",
    "xo_ref", "ko_ref", "vo_ref", "so_ref",
    "y_scr", "mix_scr", "kw_scr", "kws_scr", "vw_scr", "vws_scr", "st_scr"])


def _prompt_steady_step(s, tiles_per_seq,
                        sink_ref, x_ref, xres_ref, cos_ref, sin_ref, w_in_ref, w_out_ref, wg_ref, wu_ref, wd_ref,
                        vec_ref, gw_ref, gb_ref, abias_ref, d4_ref, rtab_ref,
                        xo_ref, ko_ref, vo_ref, so_ref,
                        y_scr, mix_scr, kw_scr, kws_scr, vw_scr, vws_scr, st_scr):
    tile = x_ref.shape[0]
    blk = PROMPT_BLOCK
    g_pre_mix, g_post_mix, g_pre_ffn, g_post_ffn, g_na, g_nb, g_ret, ln_g, ln_b = _gain_views(vec_ref)
    xi_ref, zeta_ref, dec_ref, bd_ref = _table_views(rtab_ref, blk)
    t = (s - 1) % tiles_per_seq

    @pl.when(t == 0)
    def _():
        for ref in (kw_scr, kws_scr, vw_scr, vws_scr):
            ref[0:blk, :] = jnp.zeros((blk, ref.shape[1]), BF16)
        st_scr[...] = jnp.zeros(st_scr.shape, F32)

    @pl.when(t > 0)
    def _():
        for ref in (kw_scr, kws_scr, vw_scr, vws_scr):
            ref[0:blk, :] = ref[tile:tile + blk, :]

    y_cur = y_scr.at[(s + 1) % 2]
    y_next = y_scr.at[s % 2]

    merged = _dot(mix_scr[...], w_out_ref[...])

    h = _rms(x_ref[...], g_pre_mix[...]).astype(BF16)

    k_new = y_cur[:, OFF_KA:OFF_KA + SWA_KV]
    v_new = y_cur[:, OFF_VA:OFF_VA + SWA_KV]
    kw_scr[blk:blk + tile, :] = k_new.astype(BF16)
    kws_scr[blk:blk + tile, :] = pltpu.roll(k_new, HEAD_DIM, 1).astype(BF16)
    vw_scr[blk:blk + tile, :] = _with_ones(v_new.astype(BF16))
    vws_scr[blk:blk + tile, :] = _with_ones(pltpu.roll(v_new, HEAD_DIM, 1).astype(BF16))
    ko_ref[...] = k_new[tile - WINDOW:tile]
    vo_ref[...] = v_new[tile - WINDOW:tile]

    ri = lax.broadcasted_iota(jnp.int32, (blk, blk), 0)
    ci = lax.broadcasted_iota(jnp.int32, (blk, blk), 1)
    keep = jnp.logical_not(jnp.logical_and(ri < CHUNK, ci >= CHUNK))
    w_stack = [jnp.concatenate([jnp.where(keep, gw_ref[2 * j], 0.0), jnp.where(keep, gw_ref[2 * j + 1], 0.0)],
                               axis=0).astype(BF16) for j in range(GMLP_GROUPS // 2)]

    col = lax.broadcasted_iota(jnp.int32, abias_ref.shape, 1)
    first_bias = abias_ref[...] + jnp.where(jnp.logical_and(col < blk, t == 0), NEG, 0.0)

    def mixer_pieces(j):
        r0 = j * blk
        win = pl.ds(r0, 2 * blk)
        bias = first_bias if j == 0 else abias_ref[...]

        def attn_half_fn(q, swapped):
            k_ref, v_ref = (kws_scr, vws_scr) if swapped else (kw_scr, vw_scr)
            return _swa_half(q, k_ref[win, :], v_ref[win, :], sink_ref, bias, swapped)

        gmlp_fn = lambda u, v: _gmlp_block(u, v, w_stack, gb_ref[...], ln_g[...], ln_b[...])
        ret_fn = lambda q, k, v, g: _retention_block(
            q, k, v, g, cos_ref[pl.ds(r0, blk), :], sin_ref[pl.ds(r0, blk), :], st_scr[...].astype(BF16),
            d4_ref[...], xi_ref[...], zeta_ref[...], bd_ref[...], g_ret[...])
        return _mixer_pieces(y_cur, mix_scr, r0, blk, attn_half_fn, gmlp_fn, ret_fn, g_na[...], g_nb[...])

    def advance_state(kv):
        st_scr[...] = st_scr[...] * dec_ref[...] + kv * bd_ref[...]

    assert tile // blk == 2 and len(FFN_CHUNKS) == 4
    attn_plain0, attn_swapped0, gmlp0, retention0 = mixer_pieces(0)
    attn_plain1, attn_swapped1, gmlp1, retention1 = mixer_pieces(1)
    gate_up = lambda c: _gate_up(h2, c, wg_ref, wu_ref)
    down = lambda f, act, c: _down(f, act, c, wd_ref)

    y_next[:, 0:OFF_QC] = _dot(h, w_in_ref[:, 0:OFF_QC])
    x1 = xres_ref[...] + _rms(merged, g_post_mix[...])
    h2 = _rms(x1, g_pre_ffn[...]).astype(BF16)
    act = gate_up(0)
    attn_plain0()
    f = down(None, act, 0)
    attn_swapped0()
    act = gate_up(1)
    attn_plain1()
    f = down(f, act, 1)
    attn_swapped1()
    act = gate_up(2)
    gmlp0()
    f = down(f, act, 2)
    gmlp1()
    act = gate_up(3)
    advance_state(retention0())
    advance_state(retention1())
    _store_diag_blocks(so_ref, st_scr[...])
    f = down(f, act, 3)
    y_next[:, OFF_QC:IN_WIDTH] = _dot(h, w_in_ref[:, OFF_QC:IN_WIDTH])
    xo_ref[...] = x1 + _rms(f, g_post_ffn[...])


def _sample_kernel(sink_ref, x_ref, cos_ref, sin_ref, ck_ref, cv_ref, st0_ref,
                   w_in_ref, w_out_ref, wg_ref, wu_ref, wd_ref,
                   vec_ref, gw_ref, gb_ref, d4_ref, rtab_ref,
                   xo_ref, ko_ref, vo_ref, so_ref, gv_ref,
                   y_scr, mix_scr, kw_scr, kws_scr, vw_scr, vws_scr):
    seq = CHUNK
    nseq = x_ref.shape[0] // seq
    g_pre_mix, g_post_mix, g_pre_ffn, g_post_ffn, g_na, g_nb, g_ret, ln_g, ln_b = _gain_views(vec_ref)
    xi_ref, zeta_ref, dec_ref, bd_ref = _table_views(rtab_ref, seq)

    x = x_ref[...]
    h = _rms(x, g_pre_mix[...]).astype(BF16)
    y_scr[...] = _dot(h, w_in_ref[...])

    w_stack = [jnp.concatenate([gw_ref[2 * j, 0:seq, 0:seq], gw_ref[2 * j + 1, 0:seq, 0:seq]],
                               axis=0).astype(BF16) for j in range(GMLP_GROUPS // 2)]

    for s in range(nseq):
        r0 = s * seq
        sl = pl.ds(r0, seq)
        k_new = y_scr[sl, OFF_KA:OFF_KA + SWA_KV]
        v_new = y_scr[sl, OFF_VA:OFF_VA + SWA_KV]
        ko_ref[s] = k_new
        vo_ref[s] = v_new
        k_all = jnp.concatenate([ck_ref[s], k_new], axis=0)
        v_all = jnp.concatenate([cv_ref[s], v_new], axis=0)
        kw_scr[s] = k_all.astype(BF16)
        kws_scr[s] = pltpu.roll(k_all, HEAD_DIM, 1).astype(BF16)
        vw_scr[s] = _with_ones(v_all.astype(BF16))
        vws_scr[s] = _with_ones(pltpu.roll(v_all, HEAD_DIM, 1).astype(BF16))

        st0 = st0_ref[s]
        zero = jnp.zeros((HEAD_DIM, HEAD_DIM), F32)
        state = jnp.concatenate(
            [jnp.concatenate([st0[h] if g == h else zero for g in range(RET_HEADS)], axis=1)
             for h in range(RET_HEADS)], axis=0)

        def attn_half_fn(q, swapped):
            k_ref, v_ref = (kws_scr, vws_scr) if swapped else (kw_scr, vw_scr)
            return _swa_half(q, k_ref[s], v_ref[s], sink_ref, None, swapped)

        gmlp_fn = lambda u, v: _gmlp_block(u, v, w_stack, gb_ref[...], ln_g[...], ln_b[...])
        ret_fn = lambda q, k, v, g: _retention_block(
            q, k, v, g, cos_ref[...], sin_ref[...], state.astype(BF16),
            d4_ref[...], xi_ref[...], zeta_ref[...], bd_ref[...], g_ret[...])
        attn_plain, attn_swapped, gmlp, retention = _mixer_pieces(y_scr, mix_scr, r0, seq, attn_half_fn,
                                                                  gmlp_fn, ret_fn, g_na[...], g_nb[...])
        attn_plain()
        attn_swapped()
        gv_ref[s] = gmlp()
        _store_diag_blocks(so_ref.at[s], state * dec_ref[...] + retention())

    xo_ref[...] = _ffn_tail(x, mix_scr[...], w_out_ref, wg_ref, wu_ref, wd_ref,
                            g_post_mix[...], g_pre_ffn[...], g_post_ffn[...])


def _rotary_tables(pos):
    half = HEAD_DIM // 2
    inv = ROPE_BASE ** (-np.arange(half, dtype=np.float64) / half)
    ang = np.asarray(pos, np.float64)[:, None] * inv[None, :]
    cos = np.cos(ang)
    sin = np.sin(ang)
    cos_t = np.tile(np.concatenate([cos, cos], axis=1), (1, RET_HEADS))
    sin_t = np.tile(np.concatenate([-sin, sin], axis=1), (1, RET_HEADS))
    return jnp.asarray(cos_t, F32), jnp.asarray(sin_t, F32)


def _retention_tables(block):
    logg = np.log(1.0 - 2.0 ** (-5.0 - np.arange(RET_HEADS, dtype=np.float64)))
    idx = np.arange(block, dtype=np.float64)
    rel = idx[:, None] - idx[None, :]
    decay = np.where(rel >= 0, np.exp(logg[:, None, None] * np.maximum(rel, 0.0)), 0.0)
    d4 = decay.reshape(RET_HEADS * block, block)
    xi = np.exp(logg[:, None] * (idx + 1.0))
    zeta = np.exp(logg[:, None] * (block - 1.0 - idx))
    xi_tab = np.repeat(xi.T, HEAD_DIM, axis=1)
    zeta_tab = np.repeat(zeta.T, HEAD_DIM, axis=1)
    chunk_decay = np.repeat(np.exp(logg * block), HEAD_DIM)
    hid = np.arange(RET_WIDTH) // HEAD_DIM
    bd = (hid[:, None] == hid[None, :]).astype(np.float64)
    dec_tab = bd * chunk_decay[:, None]
    return jnp.asarray(d4, F32), jnp.asarray(np.concatenate([xi_tab, zeta_tab, dec_tab, bd], axis=0), F32)


def _attn_bias(block):
    r = np.arange(block)[:, None]
    c = np.arange(2 * block)[None, :]
    visible = np.where(r < CHUNK, c < 2 * block - CHUNK, c >= CHUNK)
    return jnp.asarray(np.tile(np.where(visible, 0.0, NEG), (4, 1)), F32)


def _layer_weight(shape, layer):
    return pl.BlockSpec((None,) + shape, lambda *_: (layer,) + (0,) * len(shape), pipeline_mode=pl.Buffered(1))


def _const(shape):
    return pl.BlockSpec(shape, lambda *_: (0,) * len(shape), pipeline_mode=pl.Buffered(1))


CAST_WEIGHTS = ((D_MODEL, IN_WIDTH, 16), (D_MODEL, D_FF, 16), (D_MODEL, D_FF, 16), (D_FF, D_MODEL, 32))


def _weight_specs(layer):
    return [
        _const((D_MODEL, IN_WIDTH)),
        _layer_weight((D_MODEL, D_MODEL), layer),
        _const((D_MODEL, D_FF)),
        _const((D_MODEL, D_FF)),
        _const((D_FF, D_MODEL)),
        _layer_weight((GAIN_ROWS, D_MODEL), layer),
        _layer_weight((GMLP_GROUPS, GMLP_BLOCK, GMLP_BLOCK), layer),
    ]


def _prompt_layer(layer, x, sinks, cos_t, sin_t, weights, gb_tab, abias, ret_tabs, next_f32):
    batch, seq, _ = x.shape
    tile, blk = PROMPT_TILE, PROMPT_BLOCK
    d4, rtab = ret_tabs
    tiles_per_seq = seq // tile
    n_tiles = batch * tiles_per_seq
    stage = lambda lag: (lambda s: jnp.clip(s - lag, 0, n_tiles - 1))
    proj_tile, mix_tile, ffn_tile = stage(0), stage(1), stage(2)
    row_block = lambda tile_of: (lambda s: (tile_of(s) // tiles_per_seq, tile_of(s) % tiles_per_seq, 0))
    in_specs = [
        pl.BlockSpec(memory_space=pltpu.SMEM),
        pl.BlockSpec((None, tile, D_MODEL), row_block(proj_tile)),
        pl.BlockSpec((None, tile, D_MODEL), row_block(ffn_tile)),
        pl.BlockSpec((tile, RET_WIDTH), lambda s: (mix_tile(s) % tiles_per_seq, 0)),
        pl.BlockSpec((tile, RET_WIDTH), lambda s: (mix_tile(s) % tiles_per_seq, 0)),
        *_weight_specs(layer),
        pl.BlockSpec((None, blk, GMLP_WIDTH), lambda s: (layer, 0, 0), pipeline_mode=pl.Buffered(1)),
        _const(abias.shape), _const(d4.shape), _const(rtab.shape),
    ]
    assert len(in_specs) == N_PROMPT_INPUTS
    out_shape = [
        jax.ShapeDtypeStruct((batch, seq, D_MODEL), F32),
        jax.ShapeDtypeStruct((batch, WINDOW, SWA_KV), F32),
        jax.ShapeDtypeStruct((batch, WINDOW, SWA_KV), F32),
        jax.ShapeDtypeStruct((batch, RET_HEADS, HEAD_DIM, HEAD_DIM), F32),
    ]
    out_specs = [
        pl.BlockSpec((None, tile, D_MODEL), row_block(ffn_tile)),
        pl.BlockSpec((None, WINDOW, SWA_KV), lambda s: (mix_tile(s) // tiles_per_seq, 0, 0)),
        pl.BlockSpec((None, WINDOW, SWA_KV), lambda s: (mix_tile(s) // tiles_per_seq, 0, 0)),
        pl.BlockSpec((None, RET_HEADS, HEAD_DIM, HEAD_DIM), lambda s: (mix_tile(s) // tiles_per_seq, 0, 0, 0)),
    ]
    cast_specs = []
    for rows, cols, step in (CAST_WEIGHTS if next_f32 else ()):
        last = rows // step - 1
        assert rows % step == 0 and last < n_tiles + 2
        in_specs.append(pl.BlockSpec((None, step, cols), lambda s, last=last: (layer + 1, jnp.minimum(s, last), 0)))
        cast_specs.append(pl.BlockSpec((step, cols), lambda s, last=last: (jnp.minimum(s, last), 0)))
        out_shape.append(jax.ShapeDtypeStruct((rows, cols), BF16))
    scratch = [
        pltpu.VMEM((2, tile, IN_WIDTH), F32),
        pltpu.VMEM((tile, D_MODEL), BF16),
        pltpu.VMEM((tile + blk, SWA_KV), BF16), pltpu.VMEM((tile + blk, SWA_KV), BF16),
        pltpu.VMEM((tile + blk, 2 * SWA_KV), BF16), pltpu.VMEM((tile + blk, 2 * SWA_KV), BF16),
        pltpu.VMEM((RET_WIDTH, RET_WIDTH), F32),
    ]
    return pl.pallas_call(
        functools.partial(_prompt_kernel, tiles_per_seq, n_tiles, len(cast_specs)),
        grid=(n_tiles + 2,),
        in_specs=in_specs, out_specs=out_specs + cast_specs, out_shape=out_shape, scratch_shapes=scratch,
        compiler_params=pltpu.CompilerParams(dimension_semantics=("arbitrary",),
                                             vmem_limit_bytes=VMEM_LIMIT_BYTES),
        name=f"prompt_layer{layer}",
    )(sinks, x, x, cos_t, sin_t, *weights, gb_tab, abias, d4, rtab, *next_f32)


def _sample_layer(layer, x, sinks, cos_t, sin_t, cache_k, cache_v, state0, weights, gb_tab, ret_tabs):
    rows = x.shape[0]
    nseq = SAMPLE_SEQS
    tile = nseq * CHUNK
    n_all = rows // CHUNK
    d4, rtab = ret_tabs
    in_specs = [
        pl.BlockSpec(memory_space=pltpu.SMEM),
        pl.BlockSpec((tile, D_MODEL), lambda i: (i, 0)),
        _const(cos_t.shape), _const(sin_t.shape),
        pl.BlockSpec((nseq, None, WINDOW, SWA_KV), lambda i: (i, layer, 0, 0)),
        pl.BlockSpec((nseq, None, WINDOW, SWA_KV), lambda i: (i, layer, 0, 0)),
        pl.BlockSpec((nseq, None, RET_HEADS, HEAD_DIM, HEAD_DIM), lambda i: (i, layer, 0, 0, 0)),
        *_weight_specs(layer),
        pl.BlockSpec((None, CHUNK, GMLP_WIDTH), lambda i: (layer, 0, 0), pipeline_mode=pl.Buffered(1)),
        _const(d4.shape), _const(rtab.shape),
    ]
    out_shape = [
        jax.ShapeDtypeStruct((rows, D_MODEL), F32),
        jax.ShapeDtypeStruct((n_all, CHUNK, SWA_KV), F32),
        jax.ShapeDtypeStruct((n_all, CHUNK, SWA_KV), F32),
        jax.ShapeDtypeStruct((n_all, RET_HEADS, HEAD_DIM, HEAD_DIM), F32),
        jax.ShapeDtypeStruct((n_all, CHUNK, GMLP_WIDTH), F32),
    ]
    out_specs = [
        pl.BlockSpec((tile, D_MODEL), lambda i: (i, 0)),
        pl.BlockSpec((nseq, CHUNK, SWA_KV), lambda i: (i, 0, 0)),
        pl.BlockSpec((nseq, CHUNK, SWA_KV), lambda i: (i, 0, 0)),
        pl.BlockSpec((nseq, RET_HEADS, HEAD_DIM, HEAD_DIM), lambda i: (i, 0, 0, 0)),
        pl.BlockSpec((nseq, CHUNK, GMLP_WIDTH), lambda i: (i, 0, 0)),
    ]
    win = WINDOW + CHUNK
    scratch = [
        pltpu.VMEM((tile, IN_WIDTH), F32),
        pltpu.VMEM((tile, D_MODEL), BF16),
        pltpu.VMEM((nseq, win, SWA_KV), BF16), pltpu.VMEM((nseq, win, SWA_KV), BF16),
        pltpu.VMEM((nseq, win, 2 * SWA_KV), BF16), pltpu.VMEM((nseq, win, 2 * SWA_KV), BF16),
    ]
    return pl.pallas_call(
        _sample_kernel,
        grid=(n_all // nseq,),
        in_specs=in_specs, out_specs=out_specs, out_shape=out_shape, scratch_shapes=scratch,
        compiler_params=pltpu.CompilerParams(dimension_semantics=("arbitrary",),
                                             vmem_limit_bytes=VMEM_LIMIT_BYTES),
        name=f"sample_layer{layer}",
    )(sinks, x, cos_t, sin_t, cache_k, cache_v, state0, *weights, gb_tab, d4, rtab)


def kernel(x_prompt, x_sample, cache_swa_k, cache_swa_v, state_ret, w_in, w_out, swa_sinks,
           gmlp_w, gmlp_b, gmlp_ln_g, gmlp_ln_b, norm_a_g, norm_b_g, ret_norm_g,
           ln_pre_mix, ln_post_mix, ln_pre_ffn, ln_post_ffn, w_gate, w_up, w_down):
    batch, seq, _ = x_prompt.shape
    dec_batch, dec_seq, _ = x_sample.shape
    assert dec_seq == CHUNK and seq % PROMPT_TILE == 0 and dec_batch % SAMPLE_SEQS == 0
    assert cache_swa_k.shape[2] == WINDOW

    gains = (ln_pre_mix, ln_post_mix, ln_pre_ffn, ln_post_ffn, norm_a_g, norm_b_g, ret_norm_g, gmlp_ln_g,
             gmlp_ln_b)
    assert all(g.shape[-1] == width for g, (_, width) in zip(gains, GAIN_LAYOUT))
    vecs = jnp.stack([jnp.pad(g, ((0, 0), (0, D_MODEL - g.shape[-1]))) for g in gains], axis=1)
    vecs = jnp.pad(vecs, ((0, 0), (0, GAIN_ROWS - len(gains)), (0, 0)))
    streamed_f32 = (w_in, w_gate, w_up, w_down)
    assert all(a.shape[1:] == (rows, cols) for a, (rows, cols, _) in zip(streamed_f32, CAST_WEIGHTS))
    streamed = [a[0].astype(BF16) for a in streamed_f32]
    w_out_bf = w_out.astype(BF16)
    gb_tab = jnp.repeat(jnp.swapaxes(gmlp_b, 1, 2), HEAD_DIM, axis=2)

    cos_p, sin_p = _rotary_tables(np.arange(seq))
    cos_s, sin_s = _rotary_tables(PAST_LEN + np.arange(dec_seq))
    tabs_p = _retention_tables(PROMPT_BLOCK)
    tabs_s = _retention_tables(CHUNK)
    abias = _attn_bias(PROMPT_BLOCK)

    xs = x_sample.reshape(dec_batch * dec_seq, D_MODEL)
    ck = cache_swa_k.reshape(dec_batch, DEPTH, WINDOW, SWA_KV)
    cv = cache_swa_v.reshape(dec_batch, DEPTH, WINDOW, SWA_KV)

    xp = x_prompt
    kp_l, vp_l, ks_l, vs_l, rp_l, rs_l, gv_l = [], [], [], [], [], [], []
    for layer in range(DEPTH):
        sinks = swa_sinks[layer]
        weights = [streamed[0], w_out_bf, streamed[1], streamed[2], streamed[3], vecs, gmlp_w]
        next_f32 = streamed_f32 if layer + 1 < DEPTH else ()
        xp, kp, vp, rp, *next_streamed = _prompt_layer(layer, xp, sinks, cos_p, sin_p, weights, gb_tab, abias,
                                                       tabs_p, next_f32)
        xs, ks, vs, rs, gv = _sample_layer(layer, xs, sinks, cos_s, sin_s, ck, cv, state_ret, weights,
                                           gb_tab, tabs_s)
        streamed = next_streamed
        kp_l.append(kp); vp_l.append(vp); ks_l.append(ks); vs_l.append(vs)
        rp_l.append(rp); rs_l.append(rs); gv_l.append(gv)

    kv5 = lambda a: jnp.stack(a, axis=1).reshape(a[0].shape[0], DEPTH, a[0].shape[1], SWA_KV_HEADS, HEAD_DIM)
    return (xp, xs.reshape(dec_batch, dec_seq, D_MODEL), kv5(kp_l), kv5(vp_l), kv5(ks_l), kv5(vs_l),
            jnp.stack(rp_l, axis=1), jnp.stack(rs_l, axis=1), jnp.stack(gv_l, axis=1))
```

```python
import collections
import functools
import math

import numpy as np
import jax
import jax.numpy as jnp
from jax import lax
from jax.experimental import pallas as pl
from jax.experimental.pallas import tpu as pltpu

D_MODEL = 1024
DEPTH = 4
PAST_LEN = 2048
CHUNK = 64
HEAD_DIM = 64
SWA_HEADS = 8
SWA_KV_HEADS = 2
WINDOW = 128
SWA_Q = SWA_HEADS * HEAD_DIM
SWA_KV = SWA_KV_HEADS * HEAD_DIM
GMLP_GROUPS = 4
GMLP_BLOCK = 128
GMLP_WIDTH = GMLP_GROUPS * HEAD_DIM
RET_HEADS = 4
RET_WIDTH = RET_HEADS * HEAD_DIM
ROPE_BASE = 10000.0
IN_WIDTH = SWA_Q + 2 * SWA_KV + 2 * GMLP_WIDTH + 4 * RET_WIDTH
D_FF = 2816
EPS = 1e-6
NEG = -1e30

OFF_QA, OFF_KA, OFF_VA, OFF_UB, OFF_VB, OFF_QC, OFF_KC, OFF_VC, OFF_GC = (
    0, 512, 640, 768, 1024, 1280, 1536, 1792, 2048)

LANES = 128
PROMPT_BLOCK = 128
PROMPT_TILE = 256
SAMPLE_SEQS = 8
FFN_CHUNKS = ((0, 768), (768, 1536), (1536, 2304), (2304, D_FF))
VMEM_LIMIT_BYTES = 56 * 1024 * 1024

HEADS_PLAIN = (0, 2, 5, 7)
HEADS_SWAPPED = (1, 3, 4, 6)

F32 = jnp.float32
BF16 = jnp.bfloat16
LOG2_E = 1.0 / math.log(2.0)


def _dot(a, b):
    return jnp.dot(a, b, preferred_element_type=F32)


def _dot_nt(a, b):
    return lax.dot_general(a, b, (((1,), (1,)), ((), ())), preferred_element_type=F32)


def _rms(x, g):
    return x * lax.rsqrt(jnp.mean(x * x, axis=-1, keepdims=True) + EPS) * g


def _gelu(x):
    c = math.sqrt(2.0 / math.pi)
    return x * (0.5 * (1.0 + jnp.tanh(c * (x + 0.044715 * (x * x * x)))))


def _silu(x):
    return x / (1.0 + jnp.exp2(x * (-LOG2_E)))


def _swa_half(q, k, v_ones, sink_ref, bias, swapped):
    rows = q.shape[0]
    lo = lax.broadcasted_iota(jnp.int32, (rows, LANES), 1) < HEAD_DIM
    scale = HEAD_DIM ** -0.5 * LOG2_E

    def half(pair, keep_lo):
        qp = q[:, pair * LANES:(pair + 1) * LANES] * scale
        return jnp.where(lo if keep_lo else jnp.logical_not(lo), qp, 0.0).astype(BF16)

    lhs = jnp.concatenate([half(0, not swapped), half(1, not swapped), half(2, swapped), half(3, swapped)], axis=0)
    heads = HEADS_SWAPPED if swapped else HEADS_PLAIN
    s = _dot_nt(lhs, k)
    if bias is not None:
        s = s + bias
    ps, sink_ps = [], []
    for g, h in enumerate(heads):
        sh = s[g * rows:(g + 1) * rows]
        sink = sink_ref[h] * LOG2_E
        m = jnp.maximum(jnp.max(sh, axis=-1, keepdims=True), sink)
        ps.append(jnp.exp2(sh - m).astype(BF16))
        sink_ps.append(jnp.exp2(sink - m))
    o = _dot(jnp.concatenate(ps, axis=0), v_ones)
    return jnp.concatenate([o[g * rows:(g + 1) * rows, :LANES] / (o[g * rows:(g + 1) * rows, LANES:] + sink_ps[g])
                            for g in range(len(heads))], axis=0)


def _with_ones(v):
    return jnp.concatenate([v, jnp.ones(v.shape, v.dtype)], axis=1)


def _swa_merge(o_p, o_s):
    r = o_p.shape[0] // 4
    lo = lax.broadcasted_iota(jnp.int32, (r, LANES), 1) < HEAD_DIM
    pairs = [
        jnp.where(lo, o_p[0:r], o_s[0:r]),
        jnp.where(lo, o_p[r:2 * r], o_s[r:2 * r]),
        jnp.where(lo, o_s[2 * r:3 * r], o_p[2 * r:3 * r]),
        jnp.where(lo, o_s[3 * r:4 * r], o_p[3 * r:4 * r]),
    ]
    return jnp.concatenate(pairs, axis=1)


def _gmlp_block(u_raw, v_raw, w_stack, bias_tab, ln_g, ln_b):
    rows = u_raw.shape[0]
    u = _gelu(u_raw)
    v = _gelu(v_raw)
    mu = jnp.mean(v, axis=-1, keepdims=True)
    d = v - mu
    var = jnp.mean(d * d, axis=-1, keepdims=True)
    vn = d * lax.rsqrt(var + EPS) * ln_g + ln_b
    lo = lax.broadcasted_iota(jnp.int32, (rows, LANES), 1) < HEAD_DIM
    vb = vn.astype(BF16)
    mixed = []
    for j in range(GMLP_GROUPS // 2):
        res = _dot(w_stack[j], vb[:, j * LANES:(j + 1) * LANES])
        mixed.append(jnp.where(lo, res[0:rows], res[rows:2 * rows]))
    mixed = jnp.concatenate(mixed, axis=1) + bias_tab
    return u * mixed, vn


def _rotary(x, cos_t, sin_t):
    rows = x.shape[0]
    first = (lax.broadcasted_iota(jnp.int32, (rows, LANES), 1) % HEAD_DIM) < (HEAD_DIM // 2)
    out = []
    for j in range(RET_WIDTH // LANES):
        sl = slice(j * LANES, (j + 1) * LANES)
        xh = x[:, sl]
        partner = jnp.where(first, pltpu.roll(xh, LANES - HEAD_DIM // 2, 1), pltpu.roll(xh, HEAD_DIM // 2, 1))
        out.append(xh * cos_t[:, sl] + partner * sin_t[:, sl])
    return jnp.concatenate(out, axis=1)


def _retention_block(qc, kc, vc, gc, cos_t, sin_t, state_bf, d4, xi_tab, zeta_tab, bd_mask, ret_g):
    rows = qc.shape[0]
    qr = _rotary(qc, cos_t, sin_t)
    kr = _rotary(kc, cos_t, sin_t) * (HEAD_DIM ** -0.5)
    head = lax.broadcasted_iota(jnp.int32, (rows, RET_WIDTH), 1) // HEAD_DIM
    q4 = jnp.concatenate([jnp.where(head == h, qr, 0.0) for h in range(RET_HEADS)], axis=0).astype(BF16)
    vb = vc.astype(BF16)
    s4 = _dot_nt(q4, kr.astype(BF16)) * d4
    o4 = _dot(s4.astype(BF16), vb)
    inner = o4[0:rows]
    for h in range(1, RET_HEADS):
        inner = jnp.where(head == h, o4[h * rows:(h + 1) * rows], inner)
    cross = _dot(qr.astype(BF16), state_bf) * xi_tab
    r = inner + cross
    kz_t = jnp.transpose(kr * zeta_tab).astype(BF16)
    kv = _dot(kz_t, vb)
    avg = (bd_mask * (1.0 / HEAD_DIM)).astype(BF16)
    mu = _dot(r.astype(BF16), avg)
    d = r - mu
    var = _dot((d * d).astype(BF16), avg)
    yn = d * lax.rsqrt(var + EPS) * ret_g
    return yn * _silu(gc), kv


def _mixer_pieces(y_ref, mix_ref, r0, rows, attn_half_fn, gmlp_fn, ret_fn, g_na, g_nb):
    sl = pl.ds(r0, rows)
    held = {}

    def attention_plain():
        held["plain"] = attn_half_fn(y_ref[sl, OFF_QA:OFF_QA + SWA_Q], False)

    def attention_swapped():
        a = _swa_merge(held.pop("plain"), attn_half_fn(y_ref[sl, OFF_QA:OFF_QA + SWA_Q], True))
        mix_ref[sl, 0:SWA_Q] = _rms(a, g_na).astype(BF16)

    def gmlp():
        b, vn = gmlp_fn(y_ref[sl, OFF_UB:OFF_UB + GMLP_WIDTH], y_ref[sl, OFF_VB:OFF_VB + GMLP_WIDTH])
        mix_ref[sl, SWA_Q:SWA_Q + GMLP_WIDTH] = _rms(b, g_nb).astype(BF16)
        return vn

    def retention():
        c, kv = ret_fn(y_ref[sl, OFF_QC:OFF_QC + RET_WIDTH], y_ref[sl, OFF_KC:OFF_KC + RET_WIDTH],
                       y_ref[sl, OFF_VC:OFF_VC + RET_WIDTH], y_ref[sl, OFF_GC:OFF_GC + RET_WIDTH])
        mix_ref[sl, SWA_Q + GMLP_WIDTH:D_MODEL] = c.astype(BF16)
        return kv

    return attention_plain, attention_swapped, gmlp, retention


def _gate_up(h2, c, wg_ref, wu_ref):
    c0, c1 = FFN_CHUNKS[c]
    return (_silu(_dot(h2, wg_ref[:, c0:c1])) * _dot(h2, wu_ref[:, c0:c1])).astype(BF16)


def _down(f, act, c, wd_ref):
    c0, c1 = FFN_CHUNKS[c]
    part = _dot(act, wd_ref[c0:c1, :])
    return part if f is None else f + part


def _ffn_tail(x, mix, w_out_ref, wg_ref, wu_ref, wd_ref, g_post_mix, g_pre_ffn, g_post_ffn):
    x1 = x + _rms(_dot(mix, w_out_ref[...]), g_post_mix)
    h2 = _rms(x1, g_pre_ffn).astype(BF16)
    f = None
    for c in range(len(FFN_CHUNKS)):
        f = _down(f, _gate_up(h2, c, wg_ref, wu_ref), c, wd_ref)
    return x1 + _rms(f, g_post_ffn)


def _store_diag_blocks(out_ref, m):
    for h in range(RET_HEADS):
        out_ref[h] = m[h * HEAD_DIM:(h + 1) * HEAD_DIM, h * HEAD_DIM:(h + 1) * HEAD_DIM]


class _View:
    def __init__(self, ref, rows, cols):
        self.ref, self.rows, self.cols = ref, rows, cols

    def __getitem__(self, idx):
        assert idx is Ellipsis
        return self.ref[self.rows, self.cols]


GAIN_LAYOUT = (("pre_mix", D_MODEL), ("post_mix", D_MODEL), ("pre_ffn", D_MODEL), ("post_ffn", D_MODEL),
               ("norm_a", SWA_Q), ("norm_b", GMLP_WIDTH), ("ret_g", RET_WIDTH), ("ln_g", GMLP_WIDTH),
               ("ln_b", GMLP_WIDTH))
GAIN_ROWS = 16


def _gain_views(vec_ref):
    return [_View(vec_ref, slice(i, i + 1), slice(0, width)) for i, (_, width) in enumerate(GAIN_LAYOUT)]


def _table_views(rtab_ref, block):
    edges = (0, block, 2 * block, 2 * block + RET_WIDTH, 2 * block + 2 * RET_WIDTH)
    return [_View(rtab_ref, slice(a, b), slice(0, RET_WIDTH)) for a, b in zip(edges[:-1], edges[1:])]


N_PROMPT_INPUTS = 16
N_PROMPT_OUTPUTS = 4


def _prompt_kernel(tiles_per_seq, n_tiles, n_cast, *refs):
    n_in, n_out = N_PROMPT_INPUTS, N_PROMPT_OUTPUTS
    cast_in = refs[n_in:n_in + n_cast]
    cast_out = refs[n_in + n_cast + n_out:n_in + n_out + 2 * n_cast]
    for src, dst in zip(cast_in, cast_out):
        dst[...] = src[...].astype(BF16)
    refs = refs[:n_in] + refs[n_in + n_cast:n_in + n_cast + n_out] + refs[n_in + n_out + 2 * n_cast:]

    r = _PromptRefs(*refs)
    g_pre_mix, g_post_mix, g_pre_ffn, g_post_ffn = _gain_views(r.vec_ref)[:4]
    s = pl.program_id(0)

    @pl.when(s == 0)
    def _():
        r.mix_scr[...] = jnp.zeros(r.mix_scr.shape, BF16)
        h = _rms(r.x_ref[...], g_pre_mix[...]).astype(BF16)
        r.y_scr[0] = _dot(h, r.w_in_ref[...])

    @pl.when(jnp.logical_and(s >= 1, s <= n_tiles))
    def _():
        _prompt_steady_step(s, tiles_per_seq, *refs)

    @pl.when(s == n_tiles + 1)
    def _():
        r.xo_ref[...] = _ffn_tail(r.xres_ref[...], r.mix_scr[...], r.w_out_ref, r.wg_ref, r.wu_ref, r.wd_ref,
                                  g_post_mix[...], g_pre_ffn[...], g_post_ffn[...])


_PromptRefs = collections.namedtuple("_PromptRefs", [
    "sink_ref", "x_ref", "xres_ref", "cos_ref", "sin_ref", "w_in_ref", "w_out_ref", "wg_ref", "wu_ref", "wd_ref",
    "vec_ref", "gw_ref", "gb_ref", "abias_ref", "d4_ref", "rtab_ref",
    "xo_ref", "ko_ref", "vo_ref", "so_ref",
    "y_scr", "mix_scr", "kw_scr", "kws_scr", "vw_scr", "vws_scr", "st_scr"])


def _prompt_steady_step(s, tiles_per_seq,
                        sink_ref, x_ref, xres_ref, cos_ref, sin_ref, w_in_ref, w_out_ref, wg_ref, wu_ref, wd_ref,
                        vec_ref, gw_ref, gb_ref, abias_ref, d4_ref, rtab_ref,
                        xo_ref, ko_ref, vo_ref, so_ref,
                        y_scr, mix_scr, kw_scr, kws_scr, vw_scr, vws_scr, st_scr):
    tile = x_ref.shape[0]
    blk = PROMPT_BLOCK
    g_pre_mix, g_post_mix, g_pre_ffn, g_post_ffn, g_na, g_nb, g_ret, ln_g, ln_b = _gain_views(vec_ref)
    xi_ref, zeta_ref, dec_ref, bd_ref = _table_views(rtab_ref, blk)
    t = (s - 1) % tiles_per_seq

    @pl.when(t == 0)
    def _():
        for ref in (kw_scr, kws_scr, vw_scr, vws_scr):
            ref[0:blk, :] = jnp.zeros((blk, ref.shape[1]), BF16)
        st_scr[...] = jnp.zeros(st_scr.shape, F32)

    @pl.when(t > 0)
    def _():
        for ref in (kw_scr, kws_scr, vw_scr, vws_scr):
            ref[0:blk, :] = ref[tile:tile + blk, :]

    y_cur = y_scr.at[(s + 1) % 2]
    y_next = y_scr.at[s % 2]

    merged = _dot(mix_scr[...], w_out_ref[...])

    h = _rms(x_ref[...], g_pre_mix[...]).astype(BF16)

    k_new = y_cur[:, OFF_KA:OFF_KA + SWA_KV]
    v_new = y_cur[:, OFF_VA:OFF_VA + SWA_KV]
    kw_scr[blk:blk + tile, :] = k_new.astype(BF16)
    kws_scr[blk:blk + tile, :] = pltpu.roll(k_new, HEAD_DIM, 1).astype(BF16)
    vw_scr[blk:blk + tile, :] = _with_ones(v_new.astype(BF16))
    vws_scr[blk:blk + tile, :] = _with_ones(pltpu.roll(v_new, HEAD_DIM, 1).astype(BF16))
    ko_ref[...] = k_new[tile - WINDOW:tile]
    vo_ref[...] = v_new[tile - WINDOW:tile]

    ri = lax.broadcasted_iota(jnp.int32, (blk, blk), 0)
    ci = lax.broadcasted_iota(jnp.int32, (blk, blk), 1)
    keep = jnp.logical_not(jnp.logical_and(ri < CHUNK, ci >= CHUNK))
    w_stack = [jnp.concatenate([jnp.where(keep, gw_ref[2 * j], 0.0), jnp.where(keep, gw_ref[2 * j + 1], 0.0)],
                               axis=0).astype(BF16) for j in range(GMLP_GROUPS // 2)]

    col = lax.broadcasted_iota(jnp.int32, abias_ref.shape, 1)
    first_bias = abias_ref[...] + jnp.where(jnp.logical_and(col < blk, t == 0), NEG, 0.0)

    def mixer_pieces(j):
        r0 = j * blk
        win = pl.ds(r0, 2 * blk)
        bias = first_bias if j == 0 else abias_ref[...]

        def attn_half_fn(q, swapped):
            k_ref, v_ref = (kws_scr, vws_scr) if swapped else (kw_scr, vw_scr)
            return _swa_half(q, k_ref[win, :], v_ref[win, :], sink_ref, bias, swapped)

        gmlp_fn = lambda u, v: _gmlp_block(u, v, w_stack, gb_ref[...], ln_g[...], ln_b[...])
        ret_fn = lambda q, k, v, g: _retention_block(
            q, k, v, g, cos_ref[pl.ds(r0, blk), :], sin_ref[pl.ds(r0, blk), :], st_scr[...].astype(BF16),
            d4_ref[...], xi_ref[...], zeta_ref[...], bd_ref[...], g_ret[...])
        return _mixer_pieces(y_cur, mix_scr, r0, blk, attn_half_fn, gmlp_fn, ret_fn, g_na[...], g_nb[...])

    def advance_state(kv):
        st_scr[...] = st_scr[...] * dec_ref[...] + kv * bd_ref[...]

    assert tile // blk == 2 and len(FFN_CHUNKS) == 4
    attn_plain0, attn_swapped0, gmlp0, retention0 = mixer_pieces(0)
    attn_plain1, attn_swapped1, gmlp1, retention1 = mixer_pieces(1)
    gate_up = lambda c: _gate_up(h2, c, wg_ref, wu_ref)
    down = lambda f, act, c: _down(f, act, c, wd_ref)

    y_next[:, 0:OFF_QC] = _dot(h, w_in_ref[:, 0:OFF_QC])
    x1 = xres_ref[...] + _rms(merged, g_post_mix[...])
    h2 = _rms(x1, g_pre_ffn[...]).astype(BF16)
    act = gate_up(0)
    attn_plain0()
    f = down(None, act, 0)
    attn_swapped0()
    act = gate_up(1)
    attn_plain1()
    f = down(f, act, 1)
    attn_swapped1()
    act = gate_up(2)
    gmlp0()
    f = down(f, act, 2)
    gmlp1()
    act = gate_up(3)
    advance_state(retention0())
    advance_state(retention1())
    _store_diag_blocks(so_ref, st_scr[...])
    f = down(f, act, 3)
    y_next[:, OFF_QC:IN_WIDTH] = _dot(h, w_in_ref[:, OFF_QC:IN_WIDTH])
    xo_ref[...] = x1 + _rms(f, g_post_ffn[...])


def _sample_kernel(sink_ref, x_ref, cos_ref, sin_ref, ck_ref, cv_ref, st0_ref,
                   w_in_ref, w_out_ref, wg_ref, wu_ref, wd_ref,
                   vec_ref, gw_ref, gb_ref, d4_ref, rtab_ref,
                   xo_ref, ko_ref, vo_ref, so_ref, gv_ref,
                   y_scr, mix_scr, kw_scr, kws_scr, vw_scr, vws_scr):
    seq = CHUNK
    nseq = x_ref.shape[0] // seq
    g_pre_mix, g_post_mix, g_pre_ffn, g_post_ffn, g_na, g_nb, g_ret, ln_g, ln_b = _gain_views(vec_ref)
    xi_ref, zeta_ref, dec_ref, bd_ref = _table_views(rtab_ref, seq)

    x = x_ref[...]
    h = _rms(x, g_pre_mix[...]).astype(BF16)
    y_scr[...] = _dot(h, w_in_ref[...])

    w_stack = [jnp.concatenate([gw_ref[2 * j, 0:seq, 0:seq], gw_ref[2 * j + 1, 0:seq, 0:seq]],
                               axis=0).astype(BF16) for j in range(GMLP_GROUPS // 2)]

    for s in range(nseq):
        r0 = s * seq
        sl = pl.ds(r0, seq)
        k_new = y_scr[sl, OFF_KA:OFF_KA + SWA_KV]
        v_new = y_scr[sl, OFF_VA:OFF_VA + SWA_KV]
        ko_ref[s] = k_new
        vo_ref[s] = v_new
        k_all = jnp.concatenate([ck_ref[s], k_new], axis=0)
        v_all = jnp.concatenate([cv_ref[s], v_new], axis=0)
        kw_scr[s] = k_all.astype(BF16)
        kws_scr[s] = pltpu.roll(k_all, HEAD_DIM, 1).astype(BF16)
        vw_scr[s] = _with_ones(v_all.astype(BF16))
        vws_scr[s] = _with_ones(pltpu.roll(v_all, HEAD_DIM, 1).astype(BF16))

        st0 = st0_ref[s]
        zero = jnp.zeros((HEAD_DIM, HEAD_DIM), F32)
        state = jnp.concatenate(
            [jnp.concatenate([st0[h] if g == h else zero for g in range(RET_HEADS)], axis=1)
             for h in range(RET_HEADS)], axis=0)

        def attn_half_fn(q, swapped):
            k_ref, v_ref = (kws_scr, vws_scr) if swapped else (kw_scr, vw_scr)
            return _swa_half(q, k_ref[s], v_ref[s], sink_ref, None, swapped)

        gmlp_fn = lambda u, v: _gmlp_block(u, v, w_stack, gb_ref[...], ln_g[...], ln_b[...])
        ret_fn = lambda q, k, v, g: _retention_block(
            q, k, v, g, cos_ref[...], sin_ref[...], state.astype(BF16),
            d4_ref[...], xi_ref[...], zeta_ref[...], bd_ref[...], g_ret[...])
        attn_plain, attn_swapped, gmlp, retention = _mixer_pieces(y_scr, mix_scr, r0, seq, attn_half_fn,
                                                                  gmlp_fn, ret_fn, g_na[...], g_nb[...])
        attn_plain()
        attn_swapped()
        gv_ref[s] = gmlp()
        _store_diag_blocks(so_ref.at[s], state * dec_ref[...] + retention())

    xo_ref[...] = _ffn_tail(x, mix_scr[...], w_out_ref, wg_ref, wu_ref, wd_ref,
                            g_post_mix[...], g_pre_ffn[...], g_post_ffn[...])


def _rotary_tables(pos):
    half = HEAD_DIM // 2
    inv = ROPE_BASE ** (-np.arange(half, dtype=np.float64) / half)
    ang = np.asarray(pos, np.float64)[:, None] * inv[None, :]
    cos = np.cos(ang)
    sin = np.sin(ang)
    cos_t = np.tile(np.concatenate([cos, cos], axis=1), (1, RET_HEADS))
    sin_t = np.tile(np.concatenate([-sin, sin], axis=1), (1, RET_HEADS))
    return jnp.asarray(cos_t, F32), jnp.asarray(sin_t, F32)


def _retention_tables(block):
    logg = np.log(1.0 - 2.0 ** (-5.0 - np.arange(RET_HEADS, dtype=np.float64)))
    idx = np.arange(block, dtype=np.float64)
    rel = idx[:, None] - idx[None, :]
    decay = np.where(rel >= 0, np.exp(logg[:, None, None] * np.maximum(rel, 0.0)), 0.0)
    d4 = decay.reshape(RET_HEADS * block, block)
    xi = np.exp(logg[:, None] * (idx + 1.0))
    zeta = np.exp(logg[:, None] * (block - 1.0 - idx))
    xi_tab = np.repeat(xi.T, HEAD_DIM, axis=1)
    zeta_tab = np.repeat(zeta.T, HEAD_DIM, axis=1)
    chunk_decay = np.repeat(np.exp(logg * block), HEAD_DIM)
    hid = np.arange(RET_WIDTH) // HEAD_DIM
    bd = (hid[:, None] == hid[None, :]).astype(np.float64)
    dec_tab = bd * chunk_decay[:, None]
    return jnp.asarray(d4, F32), jnp.asarray(np.concatenate([xi_tab, zeta_tab, dec_tab, bd], axis=0), F32)


def _attn_bias(block):
    r = np.arange(block)[:, None]
    c = np.arange(2 * block)[None, :]
    visible = np.where(r < CHUNK, c < 2 * block - CHUNK, c >= CHUNK)
    return jnp.asarray(np.tile(np.where(visible, 0.0, NEG), (4, 1)), F32)


def _layer_weight(shape, layer):
    return pl.BlockSpec((None,) + shape, lambda *_: (layer,) + (0,) * len(shape), pipeline_mode=pl.Buffered(1))


def _const(shape):
    return pl.BlockSpec(shape, lambda *_: (0,) * len(shape), pipeline_mode=pl.Buffered(1))


CAST_WEIGHTS = ((D_MODEL, IN_WIDTH, 16), (D_MODEL, D_MODEL, 16), (D_MODEL, D_FF, 16), (D_MODEL, D_FF, 16),
                (D_FF, D_MODEL, 32))


def _weight_specs(layer):
    return [
        _const((D_MODEL, IN_WIDTH)),
        _const((D_MODEL, D_MODEL)),
        _const((D_MODEL, D_FF)),
        _const((D_MODEL, D_FF)),
        _const((D_FF, D_MODEL)),
        _layer_weight((GAIN_ROWS, D_MODEL), layer),
        _layer_weight((GMLP_GROUPS, GMLP_BLOCK, GMLP_BLOCK), layer),
    ]


def _prompt_layer(layer, x, sinks, cos_t, sin_t, weights, gb_tab, abias, ret_tabs, next_f32):
    batch, seq, _ = x.shape
    tile, blk = PROMPT_TILE, PROMPT_BLOCK
    d4, rtab = ret_tabs
    tiles_per_seq = seq // tile
    n_tiles = batch * tiles_per_seq
    stage = lambda lag: (lambda s: jnp.clip(s - lag, 0, n_tiles - 1))
    proj_tile, mix_tile, ffn_tile = stage(0), stage(1), stage(2)
    row_block = lambda tile_of: (lambda s: (tile_of(s) // tiles_per_seq, tile_of(s) % tiles_per_seq, 0))
    in_specs = [
        pl.BlockSpec(memory_space=pltpu.SMEM),
        pl.BlockSpec((None, tile, D_MODEL), row_block(proj_tile)),
        pl.BlockSpec((None, tile, D_MODEL), row_block(ffn_tile)),
        pl.BlockSpec((tile, RET_WIDTH), lambda s: (mix_tile(s) % tiles_per_seq, 0)),
        pl.BlockSpec((tile, RET_WIDTH), lambda s: (mix_tile(s) % tiles_per_seq, 0)),
        *_weight_specs(layer),
        pl.BlockSpec((None, blk, GMLP_WIDTH), lambda s: (layer, 0, 0), pipeline_mode=pl.Buffered(1)),
        _const(abias.shape), _const(d4.shape), _const(rtab.shape),
    ]
    assert len(in_specs) == N_PROMPT_INPUTS
    out_shape = [
        jax.ShapeDtypeStruct((batch, seq, D_MODEL), F32),
        jax.ShapeDtypeStruct((batch, WINDOW, SWA_KV), F32),
        jax.ShapeDtypeStruct((batch, WINDOW, SWA_KV), F32),
        jax.ShapeDtypeStruct((batch, RET_HEADS, HEAD_DIM, HEAD_DIM), F32),
    ]
    out_specs = [
        pl.BlockSpec((None, tile, D_MODEL), row_block(ffn_tile)),
        pl.BlockSpec((None, WINDOW, SWA_KV), lambda s: (mix_tile(s) // tiles_per_seq, 0, 0)),
        pl.BlockSpec((None, WINDOW, SWA_KV), lambda s: (mix_tile(s) // tiles_per_seq, 0, 0)),
        pl.BlockSpec((None, RET_HEADS, HEAD_DIM, HEAD_DIM), lambda s: (mix_tile(s) // tiles_per_seq, 0, 0, 0)),
    ]
    cast_specs = []
    for rows, cols, step in (CAST_WEIGHTS if next_f32 else ()):
        last = rows // step - 1
        assert rows % step == 0 and last < n_tiles + 2
        in_specs.append(pl.BlockSpec((None, step, cols), lambda s, last=last: (layer + 1, jnp.minimum(s, last), 0)))
        cast_specs.append(pl.BlockSpec((step, cols), lambda s, last=last: (jnp.minimum(s, last), 0)))
        out_shape.append(jax.ShapeDtypeStruct((rows, cols), BF16))
    scratch = [
        pltpu.VMEM((2, tile, IN_WIDTH), F32),
        pltpu.VMEM((tile, D_MODEL), BF16),
        pltpu.VMEM((tile + blk, SWA_KV), BF16), pltpu.VMEM((tile + blk, SWA_KV), BF16),
        pltpu.VMEM((tile + blk, 2 * SWA_KV), BF16), pltpu.VMEM((tile + blk, 2 * SWA_KV), BF16),
        pltpu.VMEM((RET_WIDTH, RET_WIDTH), F32),
    ]
    return pl.pallas_call(
        functools.partial(_prompt_kernel, tiles_per_seq, n_tiles, len(cast_specs)),
        grid=(n_tiles + 2,),
        in_specs=in_specs, out_specs=out_specs + cast_specs, out_shape=out_shape, scratch_shapes=scratch,
        compiler_params=pltpu.CompilerParams(dimension_semantics=("arbitrary",),
                                             vmem_limit_bytes=VMEM_LIMIT_BYTES),
        name=f"prompt_layer{layer}",
    )(sinks, x, x, cos_t, sin_t, *weights, gb_tab, abias, d4, rtab, *next_f32)


def _sample_layer(layer, x, sinks, cos_t, sin_t, cache_k, cache_v, state0, weights, gb_tab, ret_tabs):
    rows = x.shape[0]
    nseq = SAMPLE_SEQS
    tile = nseq * CHUNK
    n_all = rows // CHUNK
    d4, rtab = ret_tabs
    in_specs = [
        pl.BlockSpec(memory_space=pltpu.SMEM),
        pl.BlockSpec((tile, D_MODEL), lambda i: (i, 0)),
        _const(cos_t.shape), _const(sin_t.shape),
        pl.BlockSpec((nseq, None, WINDOW, SWA_KV), lambda i: (i, layer, 0, 0)),
        pl.BlockSpec((nseq, None, WINDOW, SWA_KV), lambda i: (i, layer, 0, 0)),
        pl.BlockSpec((nseq, None, RET_HEADS, HEAD_DIM, HEAD_DIM), lambda i: (i, layer, 0, 0, 0)),
        *_weight_specs(layer),
        pl.BlockSpec((None, CHUNK, GMLP_WIDTH), lambda i: (layer, 0, 0), pipeline_mode=pl.Buffered(1)),
        _const(d4.shape), _const(rtab.shape),
    ]
    out_shape = [
        jax.ShapeDtypeStruct((rows, D_MODEL), F32),
        jax.ShapeDtypeStruct((n_all, CHUNK, SWA_KV), F32),
        jax.ShapeDtypeStruct((n_all, CHUNK, SWA_KV), F32),
        jax.ShapeDtypeStruct((n_all, RET_HEADS, HEAD_DIM, HEAD_DIM), F32),
        jax.ShapeDtypeStruct((n_all, CHUNK, GMLP_WIDTH), F32),
    ]
    out_specs = [
        pl.BlockSpec((tile, D_MODEL), lambda i: (i, 0)),
        pl.BlockSpec((nseq, CHUNK, SWA_KV), lambda i: (i, 0, 0)),
        pl.BlockSpec((nseq, CHUNK, SWA_KV), lambda i: (i, 0, 0)),
        pl.BlockSpec((nseq, RET_HEADS, HEAD_DIM, HEAD_DIM), lambda i: (i, 0, 0, 0)),
        pl.BlockSpec((nseq, CHUNK, GMLP_WIDTH), lambda i: (i, 0, 0)),
    ]
    win = WINDOW + CHUNK
    scratch = [
        pltpu.VMEM((tile, IN_WIDTH), F32),
        pltpu.VMEM((tile, D_MODEL), BF16),
        pltpu.VMEM((nseq, win, SWA_KV), BF16), pltpu.VMEM((nseq, win, SWA_KV), BF16),
        pltpu.VMEM((nseq, win, 2 * SWA_KV), BF16), pltpu.VMEM((nseq, win, 2 * SWA_KV), BF16),
    ]
    return pl.pallas_call(
        _sample_kernel,
        grid=(n_all // nseq,),
        in_specs=in_specs, out_specs=out_specs, out_shape=out_shape, scratch_shapes=scratch,
        compiler_params=pltpu.CompilerParams(dimension_semantics=("arbitrary",),
                                             vmem_limit_bytes=VMEM_LIMIT_BYTES),
        name=f"sample_layer{layer}",
    )(sinks, x, cos_t, sin_t, cache_k, cache_v, state0, *weights, gb_tab, d4, rtab)


def kernel(x_prompt, x_sample, cache_swa_k, cache_swa_v, state_ret, w_in, w_out, swa_sinks,
           gmlp_w, gmlp_b, gmlp_ln_g, gmlp_ln_b, norm_a_g, norm_b_g, ret_norm_g,
           ln_pre_mix, ln_post_mix, ln_pre_ffn, ln_post_ffn, w_gate, w_up, w_down):
    batch, seq, _ = x_prompt.shape
    dec_batch, dec_seq, _ = x_sample.shape
    assert dec_seq == CHUNK and seq % PROMPT_TILE == 0 and dec_batch % SAMPLE_SEQS == 0
    assert cache_swa_k.shape[2] == WINDOW

    gains = (ln_pre_mix, ln_post_mix, ln_pre_ffn, ln_post_ffn, norm_a_g, norm_b_g, ret_norm_g, gmlp_ln_g,
             gmlp_ln_b)
    assert all(g.shape[-1] == width for g, (_, width) in zip(gains, GAIN_LAYOUT))
    vecs = jnp.stack([jnp.pad(g, ((0, 0), (0, D_MODEL - g.shape[-1]))) for g in gains], axis=1)
    vecs = jnp.pad(vecs, ((0, 0), (0, GAIN_ROWS - len(gains)), (0, 0)))
    streamed_f32 = (w_in, w_out, w_gate, w_up, w_down)
    assert all(a.shape[1:] == (rows, cols) for a, (rows, cols, _) in zip(streamed_f32, CAST_WEIGHTS))
    streamed = [a[0].astype(BF16) for a in streamed_f32]
    gb_tab = jnp.repeat(jnp.swapaxes(gmlp_b, 1, 2), HEAD_DIM, axis=2)

    cos_p, sin_p = _rotary_tables(np.arange(seq))
    cos_s, sin_s = _rotary_tables(PAST_LEN + np.arange(dec_seq))
    tabs_p = _retention_tables(PROMPT_BLOCK)
    tabs_s = _retention_tables(CHUNK)
    abias = _attn_bias(PROMPT_BLOCK)

    xs = x_sample.reshape(dec_batch * dec_seq, D_MODEL)
    ck = cache_swa_k.reshape(dec_batch, DEPTH, WINDOW, SWA_KV)
    cv = cache_swa_v.reshape(dec_batch, DEPTH, WINDOW, SWA_KV)

    xp = x_prompt
    kp_l, vp_l, ks_l, vs_l, rp_l, rs_l, gv_l = [], [], [], [], [], [], []
    for layer in range(DEPTH):
        sinks = swa_sinks[layer]
        weights = [*streamed, vecs, gmlp_w]
        next_f32 = streamed_f32 if layer + 1 < DEPTH else ()
        xp, kp, vp, rp, *next_streamed = _prompt_layer(layer, xp, sinks, cos_p, sin_p, weights, gb_tab, abias,
                                                       tabs_p, next_f32)
        xs, ks, vs, rs, gv = _sample_layer(layer, xs, sinks, cos_s, sin_s, ck, cv, state_ret, weights,
                                           gb_tab, tabs_s)
        streamed = next_streamed
        kp_l.append(kp); vp_l.append(vp); ks_l.append(ks); vs_l.append(vs)
        rp_l.append(rp); rs_l.append(rs); gv_l.append(gv)

    kv5 = lambda a: jnp.stack(a, axis=1).reshape(a[0].shape[0], DEPTH, a[0].shape[1], SWA_KV_HEADS, HEAD_DIM)
    return (xp, xs.reshape(dec_batch, dec_seq, D_MODEL), kv5(kp_l), kv5(vp_l), kv5(ks_l), kv5(vs_l),
            jnp.stack(rp_l, axis=1), jnp.stack(rs_l, axis=1), jnp.stack(gv_l, axis=1))
```
